```python
import jax, jax.numpy as jnp
from jax import lax
import numpy as np

D_MODEL = 1024
BATCH = 4
SEQ = 8192
DEPTH = 2

CHUNK = 64
Q_BLOCK = 128
FOX_HEADS = 8
FOX_HEAD_DIM = 64
FOX_WIDTH = FOX_HEADS * FOX_HEAD_DIM
POOL_WINDOWS = (2, 4, 8, 16)
POOL_GROUPS = len(POOL_WINDOWS)
POOL_WIDTH = D_MODEL - FOX_WIDTH
POOL_GROUP_DIM = POOL_WIDTH // POOL_GROUPS
EVEN_IN_WIDTH = 3 * FOX_WIDTH + FOX_HEADS + POOL_WIDTH
LRU_WIDTH = D_MODEL
LRU_HEADS = 4
LRU_HEAD_DIM = LRU_WIDTH // LRU_HEADS
CONV_WIDTH = 4
LRU_C = 8.0
D_FF = -(-8 * D_MODEL // (3 * 256)) * 256
RMS_EPS = 1e-6

kernel_name = "fox_pool_rglru_hybrid_trunk"


def rmsnorm(x, g):
    xf = x.astype(jnp.float32)
    y = xf * lax.rsqrt(jnp.mean(xf * xf, axis=-1, keepdims=True) + RMS_EPS)
    return (y * g.astype(jnp.float32)).astype(x.dtype)


def swiglu(h, w_gate, w_up, w_down):
    return (jax.nn.silu(h @ w_gate) * (h @ w_up)) @ w_down


def forgetting_attention(q, k, v, log_f):
    B, S, H, Dh = q.shape
    n_blk = S // Q_BLOCK
    scale = Dh ** -0.5
    c = jnp.cumsum(log_f, axis=1).transpose(0, 2, 1)
    q_blocks = q.reshape(B, n_blk, Q_BLOCK, H, Dh).transpose(1, 0, 3, 2, 4)
    c_blocks = c.reshape(B, H, n_blk, Q_BLOCK).transpose(2, 0, 1, 3)
    k_pos = jnp.arange(S, dtype=jnp.int32)

    def one_block(args):
        qb, cb, i = args
        s = jnp.einsum('bhqd,bshd->bhqs', qb, k, preferred_element_type=jnp.float32)
        s = s * scale + cb[..., None] - c[:, :, None, :]
        q_pos = i * Q_BLOCK + jnp.arange(Q_BLOCK, dtype=jnp.int32)
        mask = k_pos[None, :] <= q_pos[:, None]
        s = jnp.where(mask, s, -jnp.inf)
        p = jax.nn.softmax(s, axis=-1)
        return jnp.einsum('bhqs,bshd->bqhd', p.astype(v.dtype), v)

    out = lax.map(one_block, (q_blocks, c_blocks, jnp.arange(n_blk, dtype=jnp.int32)))
    return out.transpose(1, 0, 2, 3, 4).reshape(B, S, H * Dh)


def multiscale_pool(u, pool_w, pool_scale):
    B, S, _ = u.shape
    uf = u.astype(jnp.float32).reshape(B, S, POOL_GROUPS, POOL_GROUP_DIM)
    cs = jnp.cumsum(uf, axis=1)
    t1 = jnp.arange(1, S + 1, dtype=jnp.float32)
    pooled = []
    for g, w in enumerate(POOL_WINDOWS):
        cg = cs[:, :, g]
        lagged = jnp.pad(cg, ((0, 0), (w, 0), (0, 0)))[:, :S]
        mean = (cg - lagged) / jnp.minimum(t1, float(w))[None, :, None]
        pooled.append(mean - uf[:, :, g])
    pooled = jnp.stack(pooled, axis=2)
    mixed = jnp.einsum('bsgd,gde->bsge', pooled, pool_w.astype(jnp.float32))
    return (mixed.reshape(B, S, POOL_WIDTH) * pool_scale.astype(jnp.float32)).astype(u.dtype)


def even_mixer(h, w_in, b_f, pool_w, pool_scale, w_out):
    B, S, _ = h.shape
    proj = h @ w_in
    q, k, v, f_logit, u = jnp.split(
        proj, [FOX_WIDTH, 2 * FOX_WIDTH, 3 * FOX_WIDTH, 3 * FOX_WIDTH + FOX_HEADS], axis=-1)
    q = q.reshape(B, S, FOX_HEADS, FOX_HEAD_DIM)
    k = k.reshape(B, S, FOX_HEADS, FOX_HEAD_DIM)
    v = v.reshape(B, S, FOX_HEADS, FOX_HEAD_DIM)
    log_f = jax.nn.log_sigmoid(f_logit.astype(jnp.float32) + b_f.astype(jnp.float32))
    attn = forgetting_attention(q, k, v, log_f)
    pool = multiscale_pool(u, pool_w, pool_scale)
    return jnp.concatenate([attn, pool.astype(attn.dtype)], axis=-1) @ w_out


def causal_depthwise_conv(x, conv_w, conv_b):
    C = x.shape[-1]
    y = lax.conv_general_dilated(
        x, conv_w[:, None, :], window_strides=(1,), padding=[(CONV_WIDTH - 1, 0)],
        dimension_numbers=('NWC', 'WIO', 'NWC'), feature_group_count=C)
    return y + conv_b


def rg_lru(x, w_a, b_a, w_x, b_x, lam):
    B, S, W = x.shape
    xf = x.astype(jnp.float32)
    xh = xf.reshape(B, S, LRU_HEADS, LRU_HEAD_DIM)
    r = jax.nn.sigmoid(jnp.einsum('bshd,hde->bshe', xh, w_a.astype(jnp.float32)).reshape(B, S, W)
                       + b_a.astype(jnp.float32))
    i = jax.nn.sigmoid(jnp.einsum('bshd,hde->bshe', xh, w_x.astype(jnp.float32)).reshape(B, S, W)
                       + b_x.astype(jnp.float32))
    log_a = -LRU_C * r * jax.nn.softplus(-lam.astype(jnp.float32))
    a = jnp.exp(log_a)
    b = jnp.sqrt(-jnp.expm1(2.0 * log_a)) * (i * xf)

    def combine(c1, c2):
        a1, b1 = c1
        a2, b2 = c2
        return a1 * a2, a2 * b1 + b2

    _, hs = lax.associative_scan(combine, (a, b), axis=1)
    return hs.astype(x.dtype)


def odd_mixer(h, w_in, conv_w, conv_b, w_a, b_a, w_x, b_x, lam, w_out):
    proj = h @ w_in
    gate, xr = jnp.split(proj, 2, axis=-1)
    xr = causal_depthwise_conv(xr, conv_w, conv_b)
    y = rg_lru(xr, w_a, b_a, w_x, b_x, lam)
    return (jax.nn.gelu(gate) * y) @ w_out


def setup_inputs(seed: int = 0) -> dict:
    key = jax.random.key(seed)
    ks = jax.random.split(key, 32)
    f32 = jnp.float32
    ne = (DEPTH + 1) // 2
    no = DEPTH // 2

    def nrm(k, shape, scale):
        return jax.random.normal(k, shape, f32) * scale

    def gain(k, shape):
        return 1.0 + 0.05 * jax.random.normal(k, shape, f32)

    u = jax.random.uniform(ks[24], (no, LRU_WIDTH), f32, minval=0.9, maxval=0.999)
    s = u ** (1.0 / LRU_C)
    lam = jnp.log(s) - jnp.log1p(-s)
    return {
        "x": jax.random.normal(ks[0], (BATCH, SEQ, D_MODEL), f32),
        "mix_pre_g": gain(ks[1], (DEPTH, D_MODEL)),
        "mix_post_g": gain(ks[2], (DEPTH, D_MODEL)),
        "ffn_pre_g": gain(ks[3], (DEPTH, D_MODEL)),
        "ffn_post_g": gain(ks[4], (DEPTH, D_MODEL)),
        "ffn_w_gate": nrm(ks[5], (DEPTH, D_MODEL, D_FF), D_MODEL ** -0.5),
        "ffn_w_up": nrm(ks[6], (DEPTH, D_MODEL, D_FF), D_MODEL ** -0.5),
        "ffn_w_down": nrm(ks[7], (DEPTH, D_FF, D_MODEL), D_FF ** -0.5),
        "ev_w_in": nrm(ks[8], (ne, D_MODEL, EVEN_IN_WIDTH), D_MODEL ** -0.5),
        "ev_b_f": jax.random.uniform(ks[9], (ne, FOX_HEADS), f32, minval=1.0, maxval=6.0),
        "ev_pool_w": nrm(ks[10], (ne, POOL_GROUPS, POOL_GROUP_DIM, POOL_GROUP_DIM), POOL_GROUP_DIM ** -0.5),
        "ev_pool_scale": 1.0 + 0.1 * jax.random.normal(ks[11], (ne, POOL_WIDTH), f32),
        "ev_w_out": nrm(ks[12], (ne, D_MODEL, D_MODEL), D_MODEL ** -0.5),
        "od_w_in": nrm(ks[13], (no, D_MODEL, 2 * LRU_WIDTH), D_MODEL ** -0.5),
        "od_conv_w": nrm(ks[14], (no, CONV_WIDTH, LRU_WIDTH), CONV_WIDTH ** -0.5),
        "od_conv_b": nrm(ks[15], (no, LRU_WIDTH), 0.01),
        "od_w_a": nrm(ks[16], (no, LRU_HEADS, LRU_HEAD_DIM, LRU_HEAD_DIM), LRU_HEAD_DIM ** -0.5),
        "od_b_a": nrm(ks[17], (no, LRU_WIDTH), 0.01),
        "od_w_x": nrm(ks[18], (no, LRU_HEADS, LRU_HEAD_DIM, LRU_HEAD_DIM), LRU_HEAD_DIM ** -0.5),
        "od_b_x": nrm(ks[19], (no, LRU_WIDTH), 0.01),
        "od_lam": lam,
        "od_w_out": nrm(ks[20], (no, LRU_WIDTH, D_MODEL), LRU_WIDTH ** -0.5),
    }


def reference(x, mix_pre_g, mix_post_g, ffn_pre_g, ffn_post_g, ffn_w_gate, ffn_w_up, ffn_w_down,
              ev_w_in, ev_b_f, ev_pool_w, ev_pool_scale, ev_w_out,
              od_w_in, od_conv_w, od_conv_b, od_w_a, od_b_a, od_w_x, od_b_x, od_lam, od_w_out):
    for layer in range(DEPTH):
        h = rmsnorm(x, mix_pre_g[layer])
        if layer % 2 == 0:
            e = layer // 2
            m = even_mixer(h, ev_w_in[e], ev_b_f[e], ev_pool_w[e], ev_pool_scale[e], ev_w_out[e])
        else:
            o = layer // 2
            m = odd_mixer(h, od_w_in[o], od_conv_w[o], od_conv_b[o], od_w_a[o], od_b_a[o],
                          od_w_x[o], od_b_x[o], od_lam[o], od_w_out[o])
        x = x + rmsnorm(m, mix_post_g[layer])
        h = rmsnorm(x, ffn_pre_g[layer])
        x = x + rmsnorm(swiglu(h, ffn_w_gate[layer], ffn_w_up[layer], ffn_w_down[layer]), ffn_post_g[layer])
    return x
```

```python
import functools
import math

import jax
import jax.numpy as jnp
from jax import lax
from jax.experimental import pallas as pl
from jax.experimental.pallas import tpu as pltpu

F32 = jnp.float32
BF16 = jnp.bfloat16

RMS_EPS = 1e-6
FOX_HEADS = 8
FOX_HEAD_DIM = 64
FOX_WIDTH = FOX_HEADS * FOX_HEAD_DIM
POOL_WINDOWS = (2, 4, 8, 16)
POOL_GROUP_DIM = 128
POOL_WIDTH = len(POOL_WINDOWS) * POOL_GROUP_DIM
POOL_HALO = 16
LRU_HEADS = 4
LRU_HEAD_DIM = 256
CONV_WIDTH = 4
CONV_HALO = 8
LRU_C = 8.0

LANES = 128
SUBLANES = 8
HEAD_PAIR = 2 * FOX_HEAD_DIM
VMEM_LIMIT = 56 * 1024 * 1024


def _params(semantics):
    return pltpu.CompilerParams(dimension_semantics=semantics, vmem_limit_bytes=VMEM_LIMIT)


def _rmsnorm(x, g):
    ms = jnp.mean(x * x, axis=-1, keepdims=True)
    return x * lax.rsqrt(ms + RMS_EPS) * g


def _sigmoid(x):
    return 1.0 / (1.0 + jnp.exp(-x))


def _softplus(x):
    return jnp.maximum(x, 0.0) + jnp.log1p(jnp.exp(-jnp.abs(x)))


def _dot(a, b):
    return jnp.dot(a, b, preferred_element_type=F32)


def _ev_in_kernel(x_ref, g_ref, w_ref, wf_ref, bf_ref, qT_ref, k_ref, vT_ref, u_ref, lf_ref):
    h = _rmsnorm(x_ref[...], g_ref[...]).astype(BF16)
    proj = _dot(h, w_ref[...])
    q = proj[:, :FOX_WIDTH] * (FOX_HEAD_DIM ** -0.5)
    qT_ref[...] = q.T.astype(BF16)
    k_ref[...] = proj[:, FOX_WIDTH:2 * FOX_WIDTH].astype(BF16)
    vT_ref[...] = proj[:, 2 * FOX_WIDTH:3 * FOX_WIDTH].T.astype(BF16)
    u_ref[...] = proj[:, 3 * FOX_WIDTH:]
    f = _dot(h, wf_ref[...]) + bf_ref[...]
    lf_ref[...] = -_softplus(-f)


def _ev_in(x, g, w, wf, bf, tm):
    B, S, D = x.shape
    n = w.shape[1]
    return pl.pallas_call(
        _ev_in_kernel,
        grid=(B, S // tm),
        in_specs=[
            pl.BlockSpec((None, tm, D), lambda b, i: (b, i, 0)),
            pl.BlockSpec((1, D), lambda b, i: (0, 0)),
            pl.BlockSpec((D, n), lambda b, i: (0, 0)),
            pl.BlockSpec((D, LANES), lambda b, i: (0, 0)),
            pl.BlockSpec((1, LANES), lambda b, i: (0, 0)),
        ],
        out_specs=[
            pl.BlockSpec((None, FOX_WIDTH, tm), lambda b, i: (b, 0, i)),
            pl.BlockSpec((None, tm, FOX_WIDTH), lambda b, i: (b, i, 0)),
            pl.BlockSpec((None, FOX_WIDTH, tm), lambda b, i: (b, 0, i)),
            pl.BlockSpec((None, tm, POOL_WIDTH), lambda b, i: (b, i, 0)),
            pl.BlockSpec((None, tm, LANES), lambda b, i: (b, i, 0)),
        ],
        out_shape=[
            jax.ShapeDtypeStruct((B, FOX_WIDTH, S), BF16),
            jax.ShapeDtypeStruct((B, S, FOX_WIDTH), BF16),
            jax.ShapeDtypeStruct((B, FOX_WIDTH, S), BF16),
            jax.ShapeDtypeStruct((B, S, POOL_WIDTH), F32),
            jax.ShapeDtypeStruct((B, S, LANES), F32),
        ],
        compiler_params=_params(("parallel", "parallel")),
        name="ev_in",
    )(x, g, w, wf, bf)


def _cumsum_kernel(lf_ref, c_ref):
    n_groups = lf_ref.shape[0] // SUBLANES
    row = lax.broadcasted_iota(jnp.int32, (SUBLANES, LANES), 0)

    def body(k, carry):
        r0 = pl.multiple_of(k * SUBLANES, SUBLANES)
        blk = lf_ref[pl.ds(r0, SUBLANES), :]
        for d in (1, 2, 4):
            blk = blk + jnp.where(row >= d, pltpu.roll(blk, d, axis=0), 0.0)
        blk = blk + carry
        c_ref[pl.ds(r0, SUBLANES), :] = blk
        return jnp.broadcast_to(blk[SUBLANES - 1:SUBLANES, :], (SUBLANES, LANES))

    lax.fori_loop(0, n_groups, body, jnp.zeros((SUBLANES, LANES), F32), unroll=8)


def _fox_cumsum(lf):
    B, S, _ = lf.shape
    return pl.pallas_call(
        _cumsum_kernel,
        grid=(B,),
        in_specs=[pl.BlockSpec((None, S, LANES), lambda b: (b, 0, 0))],
        out_specs=pl.BlockSpec((None, S, LANES), lambda b: (b, 0, 0)),
        out_shape=jax.ShapeDtypeStruct((B, S, LANES), F32),
        compiler_params=_params(("parallel",)),
        name="fox_cumsum",
    )(lf)


def _attn_kernel(qT_ref, k_ref, vT_ref, c_ref, o_ref, ccol_ref, *, tq, tk):
    pair = pl.program_id(1)
    i = pl.program_id(2)
    seq = k_ref.shape[0]
    lane = lax.broadcasted_iota(jnp.int32, (1, LANES), 1)

    @pl.when(i == 0)
    def _():
        def fill(r, _):
            r0 = pl.multiple_of(r * tk, tk)
            cblk = c_ref[pl.ds(r0, tk), :]
            for hh in range(2):
                col = jnp.sum(jnp.where(lane == 2 * pair + hh, cblk, 0.0), axis=1, keepdims=True)
                ccol_ref[hh, pl.ds(r0, tk), :] = jnp.broadcast_to(col, (tk, LANES))
            return 0
        lax.fori_loop(0, seq // tk, fill, 0)

    q0 = pl.multiple_of(i * tq, tq)
    qT = qT_ref[...]
    feat = lax.broadcasted_iota(jnp.int32, (HEAD_PAIR, tq), 0)
    in_head = (feat < FOX_HEAD_DIM, feat >= FOX_HEAD_DIM)
    qz = [jnp.where(in_head[hh], qT, jnp.zeros_like(qT)) for hh in range(2)]

    cqT = c_ref[pl.ds(q0, tq), :].T
    head_row = lax.broadcasted_iota(jnp.int32, (LANES, tq), 0)
    c_t = [jnp.sum(jnp.where(head_row == 2 * pair + hh, cqT, 0.0), axis=0, keepdims=True)
           for hh in range(2)]

    def block(k0, carry, diagonal):
        k_blk = k_ref[pl.ds(k0, tk), :]
        vT_blk = vT_ref[:, pl.ds(k0, tk)]
        new = []
        for hh in range(2):
            m_old, l_old, acc_old = carry[hh]
            sT = _dot(k_blk, qz[hh])
            ccol = ccol_ref[hh, pl.ds(k0, tk), :]
            sT = sT - jnp.concatenate([ccol] * (tq // LANES), axis=1)
            if diagonal:
                key_pos = lax.broadcasted_iota(jnp.int32, (tk, tq), 0)
                qry_pos = lax.broadcasted_iota(jnp.int32, (tk, tq), 1)
                sT = jnp.where(key_pos <= qry_pos, sT, -jnp.inf)
            m_blk = jnp.max(sT, axis=0, keepdims=True) + c_t[hh]
            m_new = jnp.maximum(m_old, m_blk)
            alpha = jnp.exp(m_old - m_new)
            p = jnp.exp(sT + (c_t[hh] - m_new))
            l_new = alpha * l_old + jnp.sum(p, axis=0, keepdims=True)
            acc_new = alpha * acc_old + _dot(vT_blk, p.astype(BF16))
            new.append((m_new, l_new, acc_new))
        return tuple(new)

    init = tuple((jnp.full((1, tq), -jnp.inf, F32), jnp.zeros((1, tq), F32),
                  jnp.zeros((HEAD_PAIR, tq), F32)) for _ in range(2))
    carry = lax.fori_loop(
        0, i, lambda j, c: block(pl.multiple_of(j * tk, tk), c, False), init)
    (_, l0, acc0), (_, l1, acc1) = block(q0, carry, True)
    oT = jnp.where(in_head[0], acc0 / l0, acc1 / l1)
    o_ref[...] = oT.T.astype(o_ref.dtype)


def _fox_attn(qT, k, vT, c, tq):
    B, S, _ = k.shape
    n_pairs = FOX_WIDTH // HEAD_PAIR
    kern = functools.partial(_attn_kernel, tq=tq, tk=tq)
    return pl.pallas_call(
        kern,
        grid=(B, n_pairs, S // tq),
        in_specs=[
            pl.BlockSpec((None, HEAD_PAIR, tq), lambda b, p, i: (b, p, i)),
            pl.BlockSpec((None, S, HEAD_PAIR), lambda b, p, i: (b, 0, p)),
            pl.BlockSpec((None, HEAD_PAIR, S), lambda b, p, i: (b, p, 0)),
            pl.BlockSpec((None, S, LANES), lambda b, p, i: (b, 0, 0)),
        ],
        out_specs=pl.BlockSpec((None, tq, HEAD_PAIR), lambda b, p, i: (b, i, p)),
        out_shape=jax.ShapeDtypeStruct((B, S, FOX_WIDTH), BF16),
        scratch_shapes=[pltpu.VMEM((2, S, LANES), F32)],
        compiler_params=_params(("parallel", "parallel", "arbitrary")),
        name="fox_attn",
    )(qT, k, vT, c)


def _ev_out_kernel(x_ref, attn_ref, u_ref, uprev_ref, pw_ref, ps_ref, woa_ref, wop_ref, g_ref,
                   o_ref, *, tm):
    i = pl.program_id(1)
    u = u_ref[...]
    halo = jnp.where(i > 0, uprev_ref[...], 0.0)
    ucat = jnp.concatenate([halo, u], axis=0)
    frames = (i * tm + 1 + lax.broadcasted_iota(jnp.int32, (tm, 1), 0)).astype(F32)
    mixed = []
    for g, w in enumerate(POOL_WINDOWS):
        ug = ucat[:, g * POOL_GROUP_DIM:(g + 1) * POOL_GROUP_DIM]
        wsum = ug[POOL_HALO:]
        for back in range(1, w):
            wsum = wsum + ug[POOL_HALO - back:POOL_HALO - back + tm]
        mean = wsum / jnp.minimum(frames, float(w))
        pooled = (mean - ug[POOL_HALO:]).astype(BF16)
        mixed.append(_dot(pooled, pw_ref[g]))
    pool = jnp.concatenate(mixed, axis=1) * ps_ref[...]
    m = _dot(attn_ref[...], woa_ref[...]) + _dot(pool.astype(BF16), wop_ref[...])
    o_ref[...] = x_ref[...] + _rmsnorm(m, g_ref[...])


def _ev_out(x, attn, u, pw, ps, woa, wop, g, tm):
    B, S, D = x.shape
    halo_blocks = tm // POOL_HALO
    kern = functools.partial(_ev_out_kernel, tm=tm)
    return pl.pallas_call(
        kern,
        grid=(B, S // tm),
        in_specs=[
            pl.BlockSpec((None, tm, D), lambda b, i: (b, i, 0)),
            pl.BlockSpec((None, tm, FOX_WIDTH), lambda b, i: (b, i, 0)),
            pl.BlockSpec((None, tm, POOL_WIDTH), lambda b, i: (b, i, 0)),
            pl.BlockSpec((None, POOL_HALO, POOL_WIDTH),
                         lambda b, i: (b, jnp.maximum(i * halo_blocks - 1, 0), 0)),
            pl.BlockSpec(pw.shape, lambda b, i: (0, 0, 0)),
            pl.BlockSpec((1, POOL_WIDTH), lambda b, i: (0, 0)),
            pl.BlockSpec(woa.shape, lambda b, i: (0, 0)),
            pl.BlockSpec(wop.shape, lambda b, i: (0, 0)),
            pl.BlockSpec((1, D), lambda b, i: (0, 0)),
        ],
        out_specs=pl.BlockSpec((None, tm, D), lambda b, i: (b, i, 0)),
        out_shape=jax.ShapeDtypeStruct((B, S, D), F32),
        compiler_params=_params(("parallel", "parallel")),
        name="ev_out",
    )(x, attn, u, u, pw, ps, woa, wop, g)


def _ffn_kernel(x_ref, gpre_ref, wg_ref, wu_ref, wd_ref, gpost_ref, o_ref):
    x = x_ref[...]
    h = _rmsnorm(x, gpre_ref[...]).astype(BF16)
    gate = _dot(h, wg_ref[...])
    up = _dot(h, wu_ref[...])
    act = (gate * _sigmoid(gate) * up).astype(BF16)
    y = _dot(act, wd_ref[...])
    o_ref[...] = x + _rmsnorm(y, gpost_ref[...])


def _ffn(x, gpre, wg, wu, wd, gpost, tm):
    T, D = x.shape
    dff = wg.shape[1]
    resident = dict(pipeline_mode=pl.Buffered(1))
    return pl.pallas_call(
        _ffn_kernel,
        grid=(T // tm,),
        in_specs=[
            pl.BlockSpec((tm, D), lambda i: (i, 0)),
            pl.BlockSpec((1, D), lambda i: (0, 0)),
            pl.BlockSpec((D, dff), lambda i: (0, 0), **resident),
            pl.BlockSpec((D, dff), lambda i: (0, 0), **resident),
            pl.BlockSpec((dff, D), lambda i: (0, 0), **resident),
            pl.BlockSpec((1, D), lambda i: (0, 0)),
        ],
        out_specs=pl.BlockSpec((tm, D), lambda i: (i, 0)),
        out_shape=jax.ShapeDtypeStruct((T, D), F32),
        compiler_params=_params(("parallel",)),
        name="ffn",
    )(x, gpre, wg, wu, wd, gpost)


def _od_kernel(x_ref, gpre_ref, win_ref, cw_ref, cb_ref, wa_ref, ba_ref, wx_ref, bx_ref,
               lam_ref, wout_ref, gpost_ref, o_ref, tail_ref, state_ref, *, tm):
    width = state_ref.shape[1]

    @pl.when(pl.program_id(1) == 0)
    def _():
        tail_ref[...] = jnp.zeros_like(tail_ref)
        state_ref[...] = jnp.zeros_like(state_ref)

    x = x_ref[...]
    h = _rmsnorm(x, gpre_ref[...]).astype(BF16)
    proj = _dot(h, win_ref[...])
    gate = proj[:, :width]
    xr = proj[:, width:]

    xcat = jnp.concatenate([tail_ref[...], xr], axis=0)
    tail_ref[...] = xr[tm - CONV_HALO:]
    cw = cw_ref[...]
    xc = cb_ref[...] + cw[CONV_WIDTH - 1:CONV_WIDTH] * xr
    for back in range(1, CONV_WIDTH):
        tap = cw[CONV_WIDTH - 1 - back:CONV_WIDTH - back]
        xc = xc + tap * xcat[CONV_HALO - back:CONV_HALO - back + tm]

    xcb = xc.astype(BF16)
    r_parts, i_parts = [], []
    for hd in range(LRU_HEADS):
        sl = slice(hd * LRU_HEAD_DIM, (hd + 1) * LRU_HEAD_DIM)
        r_parts.append(_dot(xcb[:, sl], wa_ref[hd]))
        i_parts.append(_dot(xcb[:, sl], wx_ref[hd]))
    r = _sigmoid(jnp.concatenate(r_parts, axis=1) + ba_ref[...])
    ig = _sigmoid(jnp.concatenate(i_parts, axis=1) + bx_ref[...])
    log_a = (-LRU_C * _softplus(-lam_ref[...])) * r
    a = jnp.exp(log_a)
    th = jnp.tanh(log_a)
    b = jnp.sqrt(-2.0 * th / (1.0 - th)) * (ig * xc)

    row = lax.broadcasted_iota(jnp.int32, (tm, width), 0) % SUBLANES
    for d in (1, 2, 4):
        keep = row >= d
        b = jnp.where(keep, a * pltpu.roll(b, d, axis=0) + b, b)
        a = jnp.where(keep, a * pltpu.roll(a, d, axis=0), a)
    state = jnp.broadcast_to(state_ref[...], (SUBLANES, width))
    ys = []
    for grp in range(tm // SUBLANES):
        sl = slice(grp * SUBLANES, (grp + 1) * SUBLANES)
        yg = a[sl] * state + b[sl]
        ys.append(yg)
        state = jnp.broadcast_to(yg[SUBLANES - 1:SUBLANES], (SUBLANES, width))
    state_ref[...] = state[:1]
    y = jnp.concatenate(ys, axis=0)

    gelu = 0.5 * gate * (1.0 + jnp.tanh(math.sqrt(2.0 / math.pi) * (gate + 0.044715 * (gate * gate * gate))))
    m = _dot((gelu * y).astype(BF16), wout_ref[...])
    o_ref[...] = x + _rmsnorm(m, gpost_ref[...])


def _od_mixer(x, gpre, win, cw, cb, wa, ba, wx, bx, lam, wout, gpost, tm):
    B, S, D = x.shape
    width = wout.shape[0]
    kern = functools.partial(_od_kernel, tm=tm)
    vec = lambda n: pl.BlockSpec((1, n), lambda b, i: (0, 0))
    full = lambda a: pl.BlockSpec(a.shape, lambda b, i: (0,) * a.ndim)
    return pl.pallas_call(
        kern,
        grid=(B, S // tm),
        in_specs=[
            pl.BlockSpec((None, tm, D), lambda b, i: (b, i, 0)),
            vec(D), full(win), full(cw), vec(width), full(wa), vec(width), full(wx), vec(width),
            vec(width), full(wout), vec(D),
        ],
        out_specs=pl.BlockSpec((None, tm, D), lambda b, i: (b, i, 0)),
        out_shape=jax.ShapeDtypeStruct((B, S, D), F32),
        scratch_shapes=[pltpu.VMEM((CONV_HALO, width), F32), pltpu.VMEM((1, width), F32)],
        compiler_params=_params(("arbitrary", "arbitrary")),
        name="od_mixer",
    )(x, gpre, win, cw, cb, wa, ba, wx, bx, lam, wout, gpost)


def kernel(x, mix_pre_g, mix_post_g, ffn_pre_g, ffn_post_g, ffn_w_gate, ffn_w_up, ffn_w_down,
           ev_w_in, ev_b_f, ev_pool_w, ev_pool_scale, ev_w_out,
           od_w_in, od_conv_w, od_conv_b, od_w_a, od_b_a, od_w_x, od_b_x, od_lam, od_w_out):
    B, S, D = x.shape
    depth = mix_pre_g.shape[0]
    row = lambda v: v.reshape(1, -1)
    tm_rows = min(512, S)
    tq = min(256, S)

    for layer in range(depth):
        if layer % 2 == 0:
            e = layer // 2
            w_in = ev_w_in[e]
            qkv_w = 3 * FOX_WIDTH
            w_main = jnp.concatenate([w_in[:, :qkv_w], w_in[:, qkv_w + FOX_HEADS:]], axis=1).astype(BF16)
            w_f = jnp.pad(w_in[:, qkv_w:qkv_w + FOX_HEADS], ((0, 0), (0, LANES - FOX_HEADS))).astype(BF16)
            b_f = jnp.pad(ev_b_f[e], (0, LANES - FOX_HEADS)).reshape(1, LANES)
            qT, k, vT, u, lf = _ev_in(x, row(mix_pre_g[layer]), w_main, w_f, b_f, tm_rows)
            c = _fox_cumsum(lf)
            attn = _fox_attn(qT, k, vT, c, tq)
            w_out = ev_w_out[e].astype(BF16)
            x = _ev_out(x, attn, u, ev_pool_w[e].astype(BF16), row(ev_pool_scale[e]),
                        w_out[:FOX_WIDTH], w_out[FOX_WIDTH:], row(mix_post_g[layer]), tm_rows)
        else:
            o = layer // 2
            x = _od_mixer(x, row(mix_pre_g[layer]), od_w_in[o].astype(BF16), od_conv_w[o],
                          row(od_conv_b[o]), od_w_a[o].astype(BF16), row(od_b_a[o]),
                          od_w_x[o].astype(BF16), row(od_b_x[o]), row(od_lam[o]),
                          od_w_out[o].astype(BF16), row(mix_post_g[layer]), min(256, S))
        x = _ffn(x.reshape(B * S, D), row(ffn_pre_g[layer]), ffn_w_gate[layer].astype(BF16),
                 ffn_w_up[layer].astype(BF16), ffn_w_down[layer].astype(BF16),
                 row(ffn_post_g[layer]), min(256, B * S)).reshape(B, S, D)
    return x
```

```python
import functools
import math

import jax
import jax.numpy as jnp
import numpy as np
from jax import lax
from jax.experimental import pallas as pl
from jax.experimental.pallas import tpu as pltpu

F32 = jnp.float32
BF16 = jnp.bfloat16

RMS_EPS = 1e-6
FOX_HEADS = 8
FOX_HEAD_DIM = 64
FOX_WIDTH = FOX_HEADS * FOX_HEAD_DIM
POOL_WINDOWS = (2, 4, 8, 16)
POOL_GROUP_DIM = 128
POOL_WIDTH = len(POOL_WINDOWS) * POOL_GROUP_DIM
POOL_HALO = 16
LRU_HEADS = 4
LRU_HEAD_DIM = 256
CONV_WIDTH = 4
CONV_HALO = 8
LRU_C = 8.0
LOG2E = math.log2(math.e)

LANES = 128
SUBLANES = 8
HEAD_PAIR = 2 * FOX_HEAD_DIM
N_PAIRS = FOX_WIDTH // HEAD_PAIR
AUG = HEAD_PAIR + LANES
N_SPLIT = 3
BF16_SUBLANES = 16
V_ROWS = HEAD_PAIR + BF16_SUBLANES
VMEM_LIMIT = 56 * 1024 * 1024


def _params(semantics):
    return pltpu.CompilerParams(dimension_semantics=semantics, vmem_limit_bytes=VMEM_LIMIT)


def _rmsnorm(x, g):
    ms = jnp.mean(x * x, axis=-1, keepdims=True)
    return x * lax.rsqrt(ms + RMS_EPS) * g


def _sigmoid(x):
    return 1.0 / (1.0 + jnp.exp(-x))


def _softplus(x):
    return jnp.maximum(x, 0.0) + jnp.log1p(jnp.exp(-jnp.abs(x)))


def _dot(a, b):
    return jnp.dot(a, b, preferred_element_type=F32)


def _split_bf16(x):
    pieces = []
    for _ in range(N_SPLIT):
        piece = x.astype(BF16)
        pieces.append(piece)
        x = x - piece.astype(F32)
    return pieces


def _placement_constants():
    key_place = np.zeros((N_SPLIT, LANES, N_PAIRS * LANES), np.float32)
    qry_place = np.zeros((N_SPLIT, LANES, N_PAIRS * LANES), np.float32)
    key_ones = np.zeros((1, N_PAIRS * LANES), np.float32)
    qry_ones = np.zeros((1, N_PAIRS * LANES), np.float32)
    for pair in range(N_PAIRS):
        base = pair * LANES
        for hh in range(2):
            for s in range(N_SPLIT):
                key_place[s, 2 * pair + hh, base + N_SPLIT * hh + s] = -1.0
                qry_place[s, 2 * pair + hh, base + 2 * N_SPLIT + N_SPLIT * hh + s] = 1.0
        key_ones[0, base + 2 * N_SPLIT:base + 4 * N_SPLIT] = 1.0
        qry_ones[0, base:base + 2 * N_SPLIT] = 1.0
    return (jnp.asarray(key_place, BF16), jnp.asarray(qry_place, BF16),
            jnp.asarray(key_ones), jnp.asarray(qry_ones))


def _head_rows(rows, hh):
    feat = (rows >= hh * FOX_HEAD_DIM) & (rows < (hh + 1) * FOX_HEAD_DIM)
    sel = (rows >= HEAD_PAIR + N_SPLIT * hh) & (rows < HEAD_PAIR + N_SPLIT * (hh + 1))
    c_t = (rows >= HEAD_PAIR + N_SPLIT * (2 + hh)) & (rows < HEAD_PAIR + N_SPLIT * (3 + hh))
    return feat | sel | c_t


def _ev_in_kernel(x_ref, g_ref, w_ref, wf_ref, bf_ref, tri_ref, kp_ref, qp_ref, ko_ref, qo_ref,
                  qT_ref, k_ref, vT_ref, u_ref, csum_ref):
    tm = x_ref.shape[0]

    @pl.when(pl.program_id(1) == 0)
    def _():
        csum_ref[...] = jnp.zeros_like(csum_ref)

    h = _rmsnorm(x_ref[...], g_ref[...]).astype(BF16)
    proj = _dot(h, w_ref[...])
    q = proj[:, :FOX_WIDTH] * (LOG2E * FOX_HEAD_DIM ** -0.5)
    k = proj[:, FOX_WIDTH:2 * FOX_WIDTH]
    vT = proj[:, 2 * FOX_WIDTH:3 * FOX_WIDTH].T
    ones = jnp.ones((V_ROWS - HEAD_PAIR, tm), F32)
    vT_ref[...] = jnp.concatenate(
        [blk for pair in range(N_PAIRS)
         for blk in (vT[pair * HEAD_PAIR:(pair + 1) * HEAD_PAIR], ones)], axis=0).astype(BF16)
    u_ref[...] = proj[:, 3 * FOX_WIDTH:]

    f = _dot(h, wf_ref[...]) + bf_ref[...]
    log_f = -_softplus(-f)
    tri = tri_ref[...]
    c = csum_ref[...]
    for piece in _split_bf16(log_f):
        c = c + _dot(tri, piece)
    csum_ref[...] = c[tm - 1:tm, :]

    k_bias = ko_ref[...]
    q_bias = qo_ref[...]
    for s, piece in enumerate(_split_bf16(c * LOG2E)):
        k_bias = k_bias + _dot(piece, kp_ref[s])
        q_bias = q_bias + _dot(piece, qp_ref[s])
    k_cols, q_cols = [], []
    for pair in range(N_PAIRS):
        feat = slice(pair * HEAD_PAIR, (pair + 1) * HEAD_PAIR)
        bias = slice(pair * LANES, (pair + 1) * LANES)
        k_cols += [k[:, feat], k_bias[:, bias]]
        q_cols += [q[:, feat], q_bias[:, bias]]
    k_ref[...] = jnp.concatenate(k_cols, axis=1).astype(BF16)
    qT_ref[...] = jnp.concatenate(q_cols, axis=1).T.astype(BF16)


def _ev_in(x, g, w, wf, bf, tm):
    B, S, D = x.shape
    n = w.shape[1]
    tri = jnp.asarray(np.tril(np.ones((tm, tm), np.float32)), BF16)
    kp, qp, ko, qo = _placement_constants()
    const = lambda a: pl.BlockSpec(a.shape, lambda b, i: (0,) * a.ndim)
    return pl.pallas_call(
        _ev_in_kernel,
        grid=(B, S // tm),
        in_specs=[
            pl.BlockSpec((None, tm, D), lambda b, i: (b, i, 0)),
            pl.BlockSpec((1, D), lambda b, i: (0, 0)),
            pl.BlockSpec((D, n), lambda b, i: (0, 0)),
            pl.BlockSpec((D, LANES), lambda b, i: (0, 0)),
            pl.BlockSpec((1, LANES), lambda b, i: (0, 0)),
            const(tri), const(kp), const(qp), const(ko), const(qo),
        ],
        out_specs=[
            pl.BlockSpec((None, N_PAIRS * AUG, tm), lambda b, i: (b, 0, i)),
            pl.BlockSpec((None, tm, N_PAIRS * AUG), lambda b, i: (b, i, 0)),
            pl.BlockSpec((None, N_PAIRS * V_ROWS, tm), lambda b, i: (b, 0, i)),
            pl.BlockSpec((None, tm, POOL_WIDTH), lambda b, i: (b, i, 0)),
        ],
        out_shape=[
            jax.ShapeDtypeStruct((B, N_PAIRS * AUG, S), BF16),
            jax.ShapeDtypeStruct((B, S, N_PAIRS * AUG), BF16),
            jax.ShapeDtypeStruct((B, N_PAIRS * V_ROWS, S), BF16),
            jax.ShapeDtypeStruct((B, S, POOL_WIDTH), F32),
        ],
        scratch_shapes=[pltpu.VMEM((1, LANES), F32)],
        compiler_params=_params(("arbitrary", "arbitrary")),
        name="ev_in",
    )(x, g, w, wf, bf, tri, kp, qp, ko, qo)


NEG_FLOOR = -1e30


def _attn_kernel(qT_ref, k_ref, vT_ref, o_ref, s_ref, p_ref, acc_ref, *, tq, tk):
    i = pl.program_id(2)
    q_aug = qT_ref[...]
    rows = lax.broadcasted_iota(jnp.int32, (AUG, tq), 0)
    qz = [jnp.where(_head_rows(rows, hh), q_aug, jnp.zeros_like(q_aug)) for hh in range(2)]
    key_pos = lax.broadcasted_iota(jnp.int32, (tk, tq), 0)
    qry_pos = lax.broadcasted_iota(jnp.int32, (tk, tq), 1)
    acc_ref[...] = jnp.zeros_like(acc_ref)

    def stage_a(blk, slot, mask):
        k_blk = k_ref[pl.ds(pl.multiple_of(blk * tk, tk), tk), :]
        col_max = []
        for hh in range(2):
            sT = _dot(k_blk, qz[hh])
            if mask is not None:
                sT = jnp.where(mask, sT, -jnp.inf)
            s_ref[slot, hh] = sT
            col_max.append(jnp.max(sT, axis=0, keepdims=True))
        return tuple(col_max)

    def stage_b(slot, col_max, m_old):
        m, alpha = [], []
        for hh in range(2):
            m_new = jnp.maximum(m_old[hh], col_max[hh])
            alpha.append(jnp.exp2(m_old[hh] - m_new))
            p_ref[slot, hh] = jnp.exp2((s_ref[slot, hh] - m_new).astype(BF16))
            m.append(m_new)
        return tuple(m), tuple(alpha)

    def stage_c(blk, slot, alpha):
        vT_blk = vT_ref[:, pl.ds(pl.multiple_of(jnp.maximum(blk, 0) * tk, tk), tk)]
        for hh in range(2):
            acc_ref[hh] = alpha[hh] * acc_ref[hh] + _dot(vT_blk, p_ref[slot, hh])

    live = jnp.broadcast_to(i > 0, (tk, tq))
    cm0 = stage_a(0, 0, live)
    cm1 = stage_a(1, 1, live)
    m, alpha0 = stage_b(0, cm0, tuple(jnp.full((1, tq), NEG_FLOOR, F32) for _ in range(2)))

    def pair_step(u, carry):
        m, cm1, alpha0 = carry
        cm0 = stage_a(2 * u, 0, None)
        m, alpha1 = stage_b(1, cm1, m)
        stage_c(2 * u - 2, 0, alpha0)
        cm1 = stage_a(2 * u + 1, 1, None)
        m, alpha0 = stage_b(0, cm0, m)
        stage_c(2 * u - 1, 1, alpha1)
        return m, cm1, alpha0

    m, cm1, alpha0 = lax.fori_loop(1, i, pair_step, (m, cm1, alpha0))

    cm0 = stage_a(2 * i, 0, key_pos <= qry_pos)
    m, alpha1 = stage_b(1, cm1, m)
    stage_c(2 * i - 2, 0, alpha0)
    cm1 = stage_a(2 * i + 1, 1, key_pos + tk <= qry_pos)
    m, alpha0 = stage_b(0, cm0, m)
    stage_c(2 * i - 1, 1, alpha1)
    m, alpha1 = stage_b(1, cm1, m)
    stage_c(2 * i, 0, alpha0)
    stage_c(2 * i + 1, 1, alpha1)

    head0 = lax.broadcasted_iota(jnp.int32, (HEAD_PAIR, tq), 0) < FOX_HEAD_DIM
    out = [acc_ref[hh, :HEAD_PAIR] / acc_ref[hh, HEAD_PAIR:HEAD_PAIR + 1] for hh in range(2)]
    o_ref[...] = jnp.where(head0, out[0], out[1]).T.astype(o_ref.dtype)


def _fox_attn(qT, k, vT, tq):
    B, S, _ = k.shape
    tk = tq // 2
    kern = functools.partial(_attn_kernel, tq=tq, tk=tk)
    return pl.pallas_call(
        kern,
        grid=(B, N_PAIRS, S // tq),
        in_specs=[
            pl.BlockSpec((None, AUG, tq), lambda b, p, i: (b, p, i)),
            pl.BlockSpec((None, S, AUG), lambda b, p, i: (b, 0, p)),
            pl.BlockSpec((None, V_ROWS, S), lambda b, p, i: (b, p, 0)),
        ],
        out_specs=pl.BlockSpec((None, tq, HEAD_PAIR), lambda b, p, i: (b, i, p)),
        out_shape=jax.ShapeDtypeStruct((B, S, FOX_WIDTH), BF16),
        scratch_shapes=[pltpu.VMEM((2, 2, tk, tq), F32), pltpu.VMEM((2, 2, tk, tq), BF16),
                        pltpu.VMEM((2, V_ROWS, tq), F32)],
        compiler_params=_params(("parallel", "parallel", "parallel")),
        name="fox_attn",
    )(qT, k, vT)


def _ev_out_kernel(x_ref, attn_ref, u_ref, uprev_ref, pw_ref, ps_ref, woa_ref, wop_ref, g_ref,
                   o_ref, *, tm):
    i = pl.program_id(1)
    u = u_ref[...]
    halo = jnp.where(i > 0, uprev_ref[...], 0.0)
    ucat = jnp.concatenate([halo, u], axis=0)
    frames = (i * tm + 1 + lax.broadcasted_iota(jnp.int32, (tm, 1), 0)).astype(F32)
    mixed = []
    for g, w in enumerate(POOL_WINDOWS):
        ug = ucat[:, g * POOL_GROUP_DIM:(g + 1) * POOL_GROUP_DIM]
        wsum = ug[POOL_HALO:]
        for back in range(1, w):
            wsum = wsum + ug[POOL_HALO - back:POOL_HALO - back + tm]
        mean = wsum / jnp.minimum(frames, float(w))
        pooled = (mean - ug[POOL_HALO:]).astype(BF16)
        mixed.append(_dot(pooled, pw_ref[g]))
    pool = jnp.concatenate(mixed, axis=1) * ps_ref[...]
    m = _dot(attn_ref[...], woa_ref[...]) + _dot(pool.astype(BF16), wop_ref[...])
    o_ref[...] = x_ref[...] + _rmsnorm(m, g_ref[...])


def _ev_out(x, attn, u, pw, ps, woa, wop, g, tm):
    B, S, D = x.shape
    halo_blocks = tm // POOL_HALO
    kern = functools.partial(_ev_out_kernel, tm=tm)
    return pl.pallas_call(
        kern,
        grid=(B, S // tm),
        in_specs=[
            pl.BlockSpec((None, tm, D), lambda b, i: (b, i, 0)),
            pl.BlockSpec((None, tm, FOX_WIDTH), lambda b, i: (b, i, 0)),
            pl.BlockSpec((None, tm, POOL_WIDTH), lambda b, i: (b, i, 0)),
            pl.BlockSpec((None, POOL_HALO, POOL_WIDTH),
                         lambda b, i: (b, jnp.maximum(i * halo_blocks - 1, 0), 0)),
            pl.BlockSpec(pw.shape, lambda b, i: (0, 0, 0)),
            pl.BlockSpec((1, POOL_WIDTH), lambda b, i: (0, 0)),
            pl.BlockSpec(woa.shape, lambda b, i: (0, 0)),
            pl.BlockSpec(wop.shape, lambda b, i: (0, 0)),
            pl.BlockSpec((1, D), lambda b, i: (0, 0)),
        ],
        out_specs=pl.BlockSpec((None, tm, D), lambda b, i: (b, i, 0)),
        out_shape=jax.ShapeDtypeStruct((B, S, D), F32),
        compiler_params=_params(("parallel", "parallel")),
        name="ev_out",
    )(x, attn, u, u, pw, ps, woa, wop, g)


def _ffn_kernel(x_ref, gpre_ref, wg_ref, wu_ref, wd_ref, gpost_ref, o_ref):
    x = x_ref[...]
    h = _rmsnorm(x, gpre_ref[...]).astype(BF16)
    gate = _dot(h, wg_ref[...])
    up = _dot(h, wu_ref[...])
    act = (gate * _sigmoid(gate) * up).astype(BF16)
    y = _dot(act, wd_ref[...])
    o_ref[...] = x + _rmsnorm(y, gpost_ref[...])


def _ffn(x, gpre, wg, wu, wd, gpost, tm):
    T, D = x.shape
    dff = wg.shape[1]
    resident = dict(pipeline_mode=pl.Buffered(1))
    return pl.pallas_call(
        _ffn_kernel,
        grid=(T // tm,),
        in_specs=[
            pl.BlockSpec((tm, D), lambda i: (i, 0)),
            pl.BlockSpec((1, D), lambda i: (0, 0)),
            pl.BlockSpec((D, dff), lambda i: (0, 0), **resident),
            pl.BlockSpec((D, dff), lambda i: (0, 0), **resident),
            pl.BlockSpec((dff, D), lambda i: (0, 0), **resident),
            pl.BlockSpec((1, D), lambda i: (0, 0)),
        ],
        out_specs=pl.BlockSpec((tm, D), lambda i: (i, 0)),
        out_shape=jax.ShapeDtypeStruct((T, D), F32),
        compiler_params=_params(("parallel",)),
        name="ffn",
    )(x, gpre, wg, wu, wd, gpost)


def _od_kernel(x_ref, gpre_ref, win_ref, cw_ref, cb_ref, wa_ref, ba_ref, wx_ref, bx_ref,
               lam_ref, wout_ref, gpost_ref, o_ref, tail_ref, state_ref, *, tm):
    width = state_ref.shape[1]

    @pl.when(pl.program_id(1) == 0)
    def _():
        tail_ref[...] = jnp.zeros_like(tail_ref)
        state_ref[...] = jnp.zeros_like(state_ref)

    x = x_ref[...]
    h = _rmsnorm(x, gpre_ref[...]).astype(BF16)
    proj = _dot(h, win_ref[...])
    gate = proj[:, :width]
    xr = proj[:, width:]

    xcat = jnp.concatenate([tail_ref[...], xr], axis=0)
    tail_ref[...] = xr[tm - CONV_HALO:]
    cw = cw_ref[...]
    xc = cb_ref[...] + cw[CONV_WIDTH - 1:CONV_WIDTH] * xr
    for back in range(1, CONV_WIDTH):
        tap = cw[CONV_WIDTH - 1 - back:CONV_WIDTH - back]
        xc = xc + tap * xcat[CONV_HALO - back:CONV_HALO - back + tm]

    xcb = xc.astype(BF16)
    r_parts, i_parts = [], []
    for hd in range(LRU_HEADS):
        sl = slice(hd * LRU_HEAD_DIM, (hd + 1) * LRU_HEAD_DIM)
        r_parts.append(_dot(xcb[:, sl], wa_ref[hd]))
        i_parts.append(_dot(xcb[:, sl], wx_ref[hd]))
    r = _sigmoid(jnp.concatenate(r_parts, axis=1) + ba_ref[...])
    ig = _sigmoid(jnp.concatenate(i_parts, axis=1) + bx_ref[...])
    log_a = (-LRU_C * _softplus(-lam_ref[...])) * r
    a = jnp.exp(log_a)
    th = jnp.tanh(log_a)
    b = jnp.sqrt(-2.0 * th / (1.0 - th)) * (ig * xc)

    row = lax.broadcasted_iota(jnp.int32, (tm, width), 0) % SUBLANES
    for d in (1, 2, 4):
        keep = row >= d
        b = jnp.where(keep, a * pltpu.roll(b, d, axis=0) + b, b)
        a = jnp.where(keep, a * pltpu.roll(a, d, axis=0), a)
    state = jnp.broadcast_to(state_ref[...], (SUBLANES, width))
    ys = []
    for grp in range(tm // SUBLANES):
        sl = slice(grp * SUBLANES, (grp + 1) * SUBLANES)
        yg = a[sl] * state + b[sl]
        ys.append(yg)
        state = jnp.broadcast_to(yg[SUBLANES - 1:SUBLANES], (SUBLANES, width))
    state_ref[...] = state[:1]
    y = jnp.concatenate(ys, axis=0)

    gelu = 0.5 * gate * (1.0 + jnp.tanh(math.sqrt(2.0 / math.pi) * (gate + 0.044715 * (gate * gate * gate))))
    m = _dot((gelu * y).astype(BF16), wout_ref[...])
    o_ref[...] = x + _rmsnorm(m, gpost_ref[...])


def _od_mixer(x, gpre, win, cw, cb, wa, ba, wx, bx, lam, wout, gpost, tm):
    B, S, D = x.shape
    width = wout.shape[0]
    kern = functools.partial(_od_kernel, tm=tm)
    vec = lambda n: pl.BlockSpec((1, n), lambda b, i: (0, 0))
    full = lambda a: pl.BlockSpec(a.shape, lambda b, i: (0,) * a.ndim)
    return pl.pallas_call(
        kern,
        grid=(B, S // tm),
        in_specs=[
            pl.BlockSpec((None, tm, D), lambda b, i: (b, i, 0)),
            vec(D), full(win), full(cw), vec(width), full(wa), vec(width), full(wx), vec(width),
            vec(width), full(wout), vec(D),
        ],
        out_specs=pl.BlockSpec((None, tm, D), lambda b, i: (b, i, 0)),
        out_shape=jax.ShapeDtypeStruct((B, S, D), F32),
        scratch_shapes=[pltpu.VMEM((CONV_HALO, width), F32), pltpu.VMEM((1, width), F32)],
        compiler_params=_params(("arbitrary", "arbitrary")),
        name="od_mixer",
    )(x, gpre, win, cw, cb, wa, ba, wx, bx, lam, wout, gpost)


def kernel(x, mix_pre_g, mix_post_g, ffn_pre_g, ffn_post_g, ffn_w_gate, ffn_w_up, ffn_w_down,
           ev_w_in, ev_b_f, ev_pool_w, ev_pool_scale, ev_w_out,
           od_w_in, od_conv_w, od_conv_b, od_w_a, od_b_a, od_w_x, od_b_x, od_lam, od_w_out):
    B, S, D = x.shape
    depth = mix_pre_g.shape[0]
    row = lambda v: v.reshape(1, -1)
    tm_rows = min(512, S)
    tq = min(512, S)

    for layer in range(depth):
        if layer % 2 == 0:
            e = layer // 2
            w_in = ev_w_in[e]
            qkv_w = 3 * FOX_WIDTH
            w_main = jnp.concatenate([w_in[:, :qkv_w], w_in[:, qkv_w + FOX_HEADS:]], axis=1).astype(BF16)
            w_f = jnp.pad(w_in[:, qkv_w:qkv_w + FOX_HEADS], ((0, 0), (0, LANES - FOX_HEADS))).astype(BF16)
            b_f = jnp.pad(ev_b_f[e], (0, LANES - FOX_HEADS)).reshape(1, LANES)
            qT, k, vT, u = _ev_in(x, row(mix_pre_g[layer]), w_main, w_f, b_f, tm_rows)
            attn = _fox_attn(qT, k, vT, tq)
            w_out = ev_w_out[e].astype(BF16)
            x = _ev_out(x, attn, u, ev_pool_w[e].astype(BF16), row(ev_pool_scale[e]),
                        w_out[:FOX_WIDTH], w_out[FOX_WIDTH:], row(mix_post_g[layer]), tm_rows)
        else:
            o = layer // 2
            x = _od_mixer(x, row(mix_pre_g[layer]), od_w_in[o].astype(BF16), od_conv_w[o],
                          row(od_conv_b[o]), od_w_a[o].astype(BF16), row(od_b_a[o]),
                          od_w_x[o].astype(BF16), row(od_b_x[o]), row(od_lam[o]),
                          od_w_out[o].astype(BF16), row(mix_post_g[layer]), min(256, S))
        x = _ffn(x.reshape(B * S, D), row(ffn_pre_g[layer]), ffn_w_gate[layer].astype(BF16),
                 ffn_w_up[layer].astype(BF16), ffn_w_down[layer].astype(BF16),
                 row(ffn_post_g[layer]), min(256, B * S)).reshape(B, S, D)
    return x
```

```python
import functools
import math

import jax
import jax.numpy as jnp
import numpy as np
from jax import lax
from jax.experimental import pallas as pl
from jax.experimental.pallas import tpu as pltpu

F32 = jnp.float32
BF16 = jnp.bfloat16

RMS_EPS = 1e-6
FOX_HEADS = 8
FOX_HEAD_DIM = 64
FOX_WIDTH = FOX_HEADS * FOX_HEAD_DIM
POOL_WINDOWS = (2, 4, 8, 16)
POOL_GROUP_DIM = 128
POOL_WIDTH = len(POOL_WINDOWS) * POOL_GROUP_DIM
POOL_HALO = 16
LRU_HEADS = 4
LRU_HEAD_DIM = 256
CONV_WIDTH = 4
CONV_HALO = 8
LRU_C = 8.0
LOG2E = math.log2(math.e)

LANES = 128
SUBLANES = 8
HEAD_PAIR = 2 * FOX_HEAD_DIM
N_PAIRS = FOX_WIDTH // HEAD_PAIR
AUG = HEAD_PAIR + LANES
N_SPLIT = 3
ONES_LANE0 = N_SPLIT * FOX_HEADS
F32_TINY = float(np.finfo(np.float32).tiny)
VMEM_LIMIT = 56 * 1024 * 1024


def _params(semantics):
    return pltpu.CompilerParams(dimension_semantics=semantics, vmem_limit_bytes=VMEM_LIMIT)


def _rmsnorm(x, g):
    ms = jnp.mean(x * x, axis=-1, keepdims=True)
    return x * lax.rsqrt(ms + RMS_EPS) * g


def _sigmoid(x):
    return 0.5 * jnp.tanh(0.5 * x) + 0.5


def _sqrt_unit(x):
    return x * lax.rsqrt(jnp.maximum(x, F32_TINY))


def _softplus(x):
    return jnp.maximum(x, 0.0) + jnp.log1p(jnp.exp(-jnp.abs(x)))


def _dot(a, b):
    return jnp.dot(a, b, preferred_element_type=F32)


def _pack_split(x):
    packed = None
    for s in range(N_SPLIT):
        piece = x.astype(BF16).astype(F32)
        x = x - piece
        placed = piece if s == 0 else pltpu.roll(piece, s * FOX_HEADS, axis=1)
        packed = placed if packed is None else packed + placed
    return packed


def _unpack_sum(x3):
    total = x3
    for s in range(1, N_SPLIT):
        total = total + pltpu.roll(x3, LANES - s * FOX_HEADS, axis=1)
    return total


def _query_bias_constants():
    place = np.zeros((LANES, N_PAIRS * LANES), np.float32)
    ones = np.zeros((1, N_PAIRS * LANES), np.float32)
    for pair in range(N_PAIRS):
        for hh in range(2):
            for s in range(N_SPLIT):
                lane = s * FOX_HEADS + 2 * pair + hh
                place[lane, pair * LANES + ONES_LANE0 + N_SPLIT * hh + s] = 1.0
                ones[0, pair * LANES + lane] = 1.0
    return jnp.asarray(place, BF16), jnp.asarray(ones)


def _head_rows(rows, head, hh):
    feat = (rows >= hh * FOX_HEAD_DIM) & (rows < (hh + 1) * FOX_HEAD_DIM)
    bias = rows - HEAD_PAIR
    sel = (bias >= 0) & (bias < ONES_LANE0) & (jnp.bitwise_and(bias, FOX_HEADS - 1) == head)
    c_t = (bias >= ONES_LANE0 + N_SPLIT * hh) & (bias < ONES_LANE0 + N_SPLIT * (hh + 1))
    return feat | sel | c_t


def _ev_in_kernel(x_ref, g_ref, wq_ref, wk_ref, wv_ref, wu_ref, wf_ref, bf_ref, tri_ref, qp_ref, qo_ref,
                  qT_ref, qbT_ref, k_ref, kb_ref, vT_ref, u_ref, csum_ref):
    tm = x_ref.shape[0]

    @pl.when(pl.program_id(1) == 0)
    def _():
        csum_ref[...] = jnp.zeros_like(csum_ref)

    h = _rmsnorm(x_ref[...], g_ref[...]).astype(BF16)
    lane = lax.broadcasted_iota(jnp.int32, (1, LANES), 1)
    is_head = lane < FOX_HEADS

    f = _dot(h, wf_ref[...]) + bf_ref[...]
    qT_ref[...] = (_dot(h, wq_ref[...]) * (LOG2E * FOX_HEAD_DIM ** -0.5)).T.astype(BF16)

    log_f = jnp.where(is_head, -_softplus(-f), 0.0)
    c3 = _dot(tri_ref[...], _pack_split(log_f).astype(BF16))
    k_ref[...] = _dot(h, wk_ref[...]).astype(BF16)

    c = jnp.where(is_head, _unpack_sum(c3), 0.0) + csum_ref[...]
    csum_ref[...] = c[tm - 1:tm, :]
    c_packed = _pack_split(c * LOG2E)
    key_ones = ((lane >= ONES_LANE0) & (lane < ONES_LANE0 + 2 * N_SPLIT)).astype(F32)
    kb_ref[...] = (key_ones - c_packed).astype(BF16)
    q_bias = _dot(c_packed.astype(BF16), qp_ref[...]) + qo_ref[...]

    vT = _dot(h, wv_ref[...]).T
    ones = jnp.ones((FOX_HEAD_DIM, tm), F32)
    v_rows = []
    for head in range(FOX_HEADS):
        v_head = vT[head * FOX_HEAD_DIM:(head + 1) * FOX_HEAD_DIM]
        v_rows += [v_head, ones] if head % 2 == 0 else [ones, v_head]
    vT_ref[...] = jnp.concatenate(v_rows, axis=0).astype(BF16)
    qbT_ref[...] = q_bias.T.astype(BF16)
    u_ref[...] = _dot(h, wu_ref[...])


def _ev_in(x, g, w_slabs, wf, bf, tm):
    B, S, D = x.shape
    tri = jnp.asarray(np.tril(np.ones((tm, tm), np.float32)), BF16)
    qp, qo = _query_bias_constants()
    const = lambda a: pl.BlockSpec(a.shape, lambda b, i: (0,) * a.ndim)
    rows = lambda width: pl.BlockSpec((None, tm, width), lambda b, i: (b, i, 0))
    cols = lambda height: pl.BlockSpec((None, height, tm), lambda b, i: (b, 0, i))
    return pl.pallas_call(
        _ev_in_kernel,
        grid=(B, S // tm),
        in_specs=[rows(D), const(g)] + [const(w) for w in w_slabs]
                 + [const(wf), const(bf), const(tri), const(qp), const(qo)],
        out_specs=[cols(FOX_WIDTH), cols(N_PAIRS * LANES), rows(FOX_WIDTH), rows(LANES),
                   cols(FOX_HEADS * HEAD_PAIR), rows(POOL_WIDTH)],
        out_shape=[
            jax.ShapeDtypeStruct((B, FOX_WIDTH, S), BF16),
            jax.ShapeDtypeStruct((B, N_PAIRS * LANES, S), BF16),
            jax.ShapeDtypeStruct((B, S, FOX_WIDTH), BF16),
            jax.ShapeDtypeStruct((B, S, LANES), BF16),
            jax.ShapeDtypeStruct((B, FOX_HEADS * HEAD_PAIR, S), BF16),
            jax.ShapeDtypeStruct((B, S, POOL_WIDTH), F32),
        ],
        scratch_shapes=[pltpu.VMEM((1, LANES), F32)],
        compiler_params=_params(("arbitrary", "arbitrary")),
        name="ev_in",
    )(x, g, *w_slabs, wf, bf, tri, qp, qo)


NEG_FLOOR = -1e30


def _attn_kernel(qT_ref, qbT_ref, k_ref, kb_ref, vT_ref, o_ref, s_ref, p_ref, acc_ref, *, tq, tk):
    pair = pl.program_id(1)
    i = pl.program_id(2)
    q_aug = jnp.concatenate([qT_ref[...], qbT_ref[...]], axis=0)
    rows = lax.broadcasted_iota(jnp.int32, (AUG, tq), 0)
    qz = [jnp.where(_head_rows(rows, 2 * pair + hh, hh), q_aug, jnp.zeros_like(q_aug))
          for hh in range(2)]
    key_pos = lax.broadcasted_iota(jnp.int32, (tk, tq), 0)
    qry_pos = lax.broadcasted_iota(jnp.int32, (tk, tq), 1)
    acc_ref[...] = jnp.zeros_like(acc_ref)

    def stage_a(blk, slot, mask):
        keys = pl.ds(pl.multiple_of(blk * tk, tk), tk)
        k_blk = jnp.concatenate([k_ref[keys, :], kb_ref[keys, :]], axis=1)
        col_max = []
        for hh in range(2):
            sT = _dot(k_blk, qz[hh])
            if mask is not None:
                sT = jnp.where(mask, sT, -jnp.inf)
            s_ref[slot, hh] = sT
            col_max.append(jnp.max(sT, axis=0, keepdims=True))
        return tuple(col_max)

    def stage_b(slot, col_max, m_old):
        m, alpha = [], []
        for hh in range(2):
            m_new = jnp.maximum(m_old[hh], col_max[hh])
            alpha.append(jnp.exp2(m_old[hh] - m_new))
            p_ref[slot, hh] = jnp.exp2((s_ref[slot, hh] - m_new).astype(BF16))
            m.append(m_new)
        return tuple(m), tuple(alpha)

    def stage_c(blk, slot, alpha):
        keys = pl.ds(pl.multiple_of(jnp.maximum(blk, 0) * tk, tk), tk)
        for hh in range(2):
            vT_blk = vT_ref[hh * HEAD_PAIR:(hh + 1) * HEAD_PAIR, keys]
            acc_ref[hh] = alpha[hh] * acc_ref[hh] + _dot(vT_blk, p_ref[slot, hh])

    live = jnp.broadcast_to(i > 0, (tk, tq))
    cm0 = stage_a(0, 0, live)
    cm1 = stage_a(1, 1, live)
    m, alpha0 = stage_b(0, cm0, tuple(jnp.full((1, tq), NEG_FLOOR, F32) for _ in range(2)))

    def pair_step(u, carry):
        m, cm1, alpha0 = carry
        cm0 = stage_a(2 * u, 0, None)
        m, alpha1 = stage_b(1, cm1, m)
        stage_c(2 * u - 2, 0, alpha0)
        cm1 = stage_a(2 * u + 1, 1, None)
        m, alpha0 = stage_b(0, cm0, m)
        stage_c(2 * u - 1, 1, alpha1)
        return m, cm1, alpha0

    m, cm1, alpha0 = lax.fori_loop(1, i, pair_step, (m, cm1, alpha0))

    cm0 = stage_a(2 * i, 0, key_pos <= qry_pos)
    m, alpha1 = stage_b(1, cm1, m)
    stage_c(2 * i - 2, 0, alpha0)
    cm1 = stage_a(2 * i + 1, 1, key_pos + tk <= qry_pos)
    m, alpha0 = stage_b(0, cm0, m)
    stage_c(2 * i - 1, 1, alpha1)
    m, alpha1 = stage_b(1, cm1, m)
    stage_c(2 * i, 0, alpha0)
    stage_c(2 * i + 1, 1, alpha1)

    out0 = acc_ref[0, :FOX_HEAD_DIM] / acc_ref[0, FOX_HEAD_DIM:FOX_HEAD_DIM + 1]
    out1 = acc_ref[1, FOX_HEAD_DIM:] / acc_ref[1, :1]
    o_ref[...] = jnp.concatenate([out0, out1], axis=0).T.astype(o_ref.dtype)


def _fox_attn(qT, qbT, k, kb, vT, tq):
    B, S, _ = k.shape
    tk = tq // 2
    kern = functools.partial(_attn_kernel, tq=tq, tk=tk)
    return pl.pallas_call(
        kern,
        grid=(B, N_PAIRS, S // tq),
        in_specs=[
            pl.BlockSpec((None, HEAD_PAIR, tq), lambda b, p, i: (b, p, i)),
            pl.BlockSpec((None, LANES, tq), lambda b, p, i: (b, p, i)),
            pl.BlockSpec((None, S, HEAD_PAIR), lambda b, p, i: (b, 0, p)),
            pl.BlockSpec((None, S, LANES), lambda b, p, i: (b, 0, 0)),
            pl.BlockSpec((None, 2 * HEAD_PAIR, S), lambda b, p, i: (b, p, 0)),
        ],
        out_specs=pl.BlockSpec((None, tq, HEAD_PAIR), lambda b, p, i: (b, i, p)),
        out_shape=jax.ShapeDtypeStruct((B, S, FOX_WIDTH), BF16),
        scratch_shapes=[pltpu.VMEM((2, 2, tk, tq), F32), pltpu.VMEM((2, 2, tk, tq), BF16),
                        pltpu.VMEM((2, HEAD_PAIR, tq), F32)],
        compiler_params=_params(("parallel", "parallel", "parallel")),
        name="fox_attn",
    )(qT, qbT, k, kb, vT)


def _ev_out_kernel(x_ref, attn_ref, u_ref, uprev_ref, pw_ref, ps_ref, woa_ref, wop_ref, g_ref,
                   o_ref, *, tm):
    i = pl.program_id(1)
    u = u_ref[...]
    halo = jnp.where(i > 0, uprev_ref[...], 0.0)
    ucat = jnp.concatenate([halo, u], axis=0)
    frames = (i * tm + 1 + lax.broadcasted_iota(jnp.int32, (tm, 1), 0)).astype(F32)
    mixed = []
    for g, w in enumerate(POOL_WINDOWS):
        ug = ucat[:, g * POOL_GROUP_DIM:(g + 1) * POOL_GROUP_DIM]
        wsum = ug[POOL_HALO:]
        for back in range(1, w):
            wsum = wsum + ug[POOL_HALO - back:POOL_HALO - back + tm]
        mean = wsum / jnp.minimum(frames, float(w))
        pooled = (mean - ug[POOL_HALO:]).astype(BF16)
        mixed.append(_dot(pooled, pw_ref[g]))
    pool = jnp.concatenate(mixed, axis=1) * ps_ref[...]
    m = _dot(attn_ref[...], woa_ref[...]) + _dot(pool.astype(BF16), wop_ref[...])
    o_ref[...] = x_ref[...] + _rmsnorm(m, g_ref[...])


def _ev_out(x, attn, u, pw, ps, woa, wop, g, tm):
    B, S, D = x.shape
    halo_blocks = tm // POOL_HALO
    kern = functools.partial(_ev_out_kernel, tm=tm)
    return pl.pallas_call(
        kern,
        grid=(B, S // tm),
        in_specs=[
            pl.BlockSpec((None, tm, D), lambda b, i: (b, i, 0)),
            pl.BlockSpec((None, tm, FOX_WIDTH), lambda b, i: (b, i, 0)),
            pl.BlockSpec((None, tm, POOL_WIDTH), lambda b, i: (b, i, 0)),
            pl.BlockSpec((None, POOL_HALO, POOL_WIDTH),
                         lambda b, i: (b, jnp.maximum(i * halo_blocks - 1, 0), 0)),
            pl.BlockSpec(pw.shape, lambda b, i: (0, 0, 0)),
            pl.BlockSpec((1, POOL_WIDTH), lambda b, i: (0, 0)),
            pl.BlockSpec(woa.shape, lambda b, i: (0, 0)),
            pl.BlockSpec(wop.shape, lambda b, i: (0, 0)),
            pl.BlockSpec((1, D), lambda b, i: (0, 0)),
        ],
        out_specs=pl.BlockSpec((None, tm, D), lambda b, i: (b, i, 0)),
        out_shape=jax.ShapeDtypeStruct((B, S, D), F32),
        compiler_params=_params(("parallel", "parallel")),
        name="ev_out",
    )(x, attn, u, u, pw, ps, woa, wop, g)


def _ffn_stages(x, gpre_ref, wg_ref, wu_ref, wd_ref, gpost_ref):
    h = _rmsnorm(x, gpre_ref[...]).astype(BF16)
    gate = _dot(h, wg_ref[...])
    yield None
    up = _dot(h, wu_ref[...])
    yield None
    act = (gate * _sigmoid(gate) * up).astype(BF16)
    y = _dot(act, wd_ref[...])
    yield None
    yield x + _rmsnorm(y, gpost_ref[...])


def _ffn_kernel(x_ref, gpre_ref, wg_ref, wu_ref, wd_ref, gpost_ref, o_ref):
    *_, out = _ffn_stages(x_ref[...], gpre_ref, wg_ref, wu_ref, wd_ref, gpost_ref)
    o_ref[...] = out


def _ffn(x, gpre, wg, wu, wd, gpost, tm):
    T, D = x.shape
    dff = wg.shape[1]
    resident = dict(pipeline_mode=pl.Buffered(1))
    return pl.pallas_call(
        _ffn_kernel,
        grid=(T // tm,),
        in_specs=[
            pl.BlockSpec((tm, D), lambda i: (i, 0)),
            pl.BlockSpec((1, D), lambda i: (0, 0)),
            pl.BlockSpec((D, dff), lambda i: (0, 0), **resident),
            pl.BlockSpec((D, dff), lambda i: (0, 0), **resident),
            pl.BlockSpec((dff, D), lambda i: (0, 0), **resident),
            pl.BlockSpec((1, D), lambda i: (0, 0)),
        ],
        out_specs=pl.BlockSpec((tm, D), lambda i: (i, 0)),
        out_shape=jax.ShapeDtypeStruct((T, D), F32),
        compiler_params=_params(("parallel",)),
        name="ffn",
    )(x, gpre, wg, wu, wd, gpost)


def _od_stages(x, gpre_ref, win_ref, cw_ref, cb_ref, wa_ref, ba_ref, wx_ref, bx_ref,
               lam_ref, wout_ref, gpost_ref, tail_ref, state_ref, a_ref, b_ref):
    tm = x.shape[0]
    width = state_ref.shape[1]
    h = _rmsnorm(x, gpre_ref[...]).astype(BF16)
    proj = _dot(h, win_ref[...])
    yield None
    gate = proj[:, :width]
    xr = proj[:, width:]

    xcat = jnp.concatenate([tail_ref[...], xr], axis=0)
    tail_ref[...] = xr[tm - CONV_HALO:]
    cw = cw_ref[...]
    xc = cb_ref[...] + cw[CONV_WIDTH - 1:CONV_WIDTH] * xr
    for back in range(1, CONV_WIDTH):
        tap = cw[CONV_WIDTH - 1 - back:CONV_WIDTH - back]
        xc = xc + tap * xcat[CONV_HALO - back:CONV_HALO - back + tm]

    xcb = xc.astype(BF16)
    r_parts, i_parts = [], []
    for hd in range(LRU_HEADS):
        sl = slice(hd * LRU_HEAD_DIM, (hd + 1) * LRU_HEAD_DIM)
        r_parts.append(_dot(xcb[:, sl], wa_ref[hd]))
        i_parts.append(_dot(xcb[:, sl], wx_ref[hd]))
    yield None
    r = _sigmoid(jnp.concatenate(r_parts, axis=1) + ba_ref[...])
    ig = _sigmoid(jnp.concatenate(i_parts, axis=1) + bx_ref[...])
    log_a = (-LRU_C * _softplus(-lam_ref[...])) * r
    a = jnp.exp(log_a)
    th = jnp.tanh(log_a)
    b = _sqrt_unit(-2.0 * th / (1.0 - th)) * (ig * xc)

    groups = tm // SUBLANES
    grp = lax.broadcasted_iota(jnp.int32, (groups, LANES), 0)
    pos = lambda ref, c, r: ref.at[c, pl.ds(r, groups, stride=SUBLANES), :]
    y_cols = []
    for c in range(width // LANES):
        lanes = slice(c * LANES, (c + 1) * LANES)
        a_ref[c] = a[:, lanes]
        b_ref[c] = b[:, lanes]
        a_pos = [pos(a_ref, c, r)[...] for r in range(SUBLANES)]
        b_pos = [pos(b_ref, c, r)[...] for r in range(SUBLANES)]
        for r in range(1, SUBLANES):
            b_pos[r] = a_pos[r] * b_pos[r - 1] + b_pos[r]
            a_pos[r] = a_pos[r] * a_pos[r - 1]
        ga, gb = a_pos[SUBLANES - 1], b_pos[SUBLANES - 1]
        d = 1
        while d < groups:
            keep = grp >= d
            gb = jnp.where(keep, ga * pltpu.roll(gb, d, axis=0) + gb, gb)
            ga = jnp.where(keep, ga * pltpu.roll(ga, d, axis=0), ga)
            d *= 2
        state = state_ref[:, lanes]
        h_end = ga * state + gb
        carry_in = jnp.where(grp >= 1, pltpu.roll(h_end, 1, axis=0), state)
        state_ref[:, lanes] = h_end[groups - 1:groups]
        for r in range(SUBLANES):
            pos(b_ref, c, r)[...] = a_pos[r] * carry_in + b_pos[r]
        y_cols.append(b_ref[c])
    y = jnp.concatenate(y_cols, axis=1)

    gelu = 0.5 * gate * (1.0 + jnp.tanh(math.sqrt(2.0 / math.pi) * (gate + 0.044715 * (gate * gate * gate))))
    m = _dot((gelu * y).astype(BF16), wout_ref[...])
    yield None
    yield x + _rmsnorm(m, gpost_ref[...])


def _od_ffn_kernel(x_ref, *refs, n_seq, n_tiles):
    od_refs, ffn_refs = refs[:11], refs[11:16]
    o_ref, mid_ref, tail_ref, state_ref, a_ref, b_ref = refs[16:]
    t = pl.program_id(0)

    @pl.when(t == 0)
    def _():
        mid_ref[...] = jnp.zeros_like(mid_ref)

    @pl.when(t % n_seq == 0)
    def _():
        tail_ref[...] = jnp.zeros_like(tail_ref)
        state_ref[...] = jnp.zeros_like(state_ref)

    ffn = _ffn_stages(mid_ref[(t + 1) % 2], *ffn_refs)
    od = _od_stages(x_ref[...], *od_refs, tail_ref, state_ref, a_ref, b_ref)
    for gen in (od, ffn, od, ffn, ffn, od):
        next(gen)
    o_ref[...] = next(ffn)
    mid_ref[t % 2] = next(od)


def _od_ffn(x, od_params, ffn_params, tm):
    B, S, D = x.shape
    n_seq = S // tm
    n_tiles = B * n_seq
    width = od_params[-2].shape[0]
    kern = functools.partial(_od_ffn_kernel, n_seq=n_seq, n_tiles=n_tiles)
    resident = lambda a: pl.BlockSpec(a.shape, lambda t: (0,) * a.ndim, pipeline_mode=pl.Buffered(1))
    params = list(od_params) + list(ffn_params)
    out = pl.pallas_call(
        kern,
        grid=(n_tiles + 1,),
        in_specs=[pl.BlockSpec((tm, D), lambda t: (jnp.minimum(t, n_tiles - 1), 0))]
                 + [resident(a) for a in params],
        out_specs=pl.BlockSpec((tm, D), lambda t: (jnp.maximum(t - 1, 0), 0)),
        out_shape=jax.ShapeDtypeStruct((B * S, D), F32),
        scratch_shapes=[pltpu.VMEM((2, tm, D), F32), pltpu.VMEM((CONV_HALO, width), F32),
                        pltpu.VMEM((1, width), F32), pltpu.VMEM((width // LANES, tm, LANES), F32),
                        pltpu.VMEM((width // LANES, tm, LANES), F32)],
        compiler_params=_params(("arbitrary",)),
        name="od_ffn",
    )(x.reshape(B * S, D), *params)
    return out.reshape(B, S, D)


def kernel(x, mix_pre_g, mix_post_g, ffn_pre_g, ffn_post_g, ffn_w_gate, ffn_w_up, ffn_w_down,
           ev_w_in, ev_b_f, ev_pool_w, ev_pool_scale, ev_w_out,
           od_w_in, od_conv_w, od_conv_b, od_w_a, od_b_a, od_w_x, od_b_x, od_lam, od_w_out):
    B, S, D = x.shape
    depth = mix_pre_g.shape[0]
    row = lambda v: v.reshape(1, -1)
    tm_rows = min(512, S)
    tq = min(512, S)

    for layer in range(depth):
        if layer % 2 == 0:
            e = layer // 2
            w_in = ev_w_in[e]
            qkv_w = 3 * FOX_WIDTH
            w_main = [w_in[:, n * FOX_WIDTH:(n + 1) * FOX_WIDTH].astype(BF16) for n in range(3)]
            w_main.append(w_in[:, qkv_w + FOX_HEADS:].astype(BF16))
            w_f = jnp.pad(w_in[:, qkv_w:qkv_w + FOX_HEADS], ((0, 0), (0, LANES - FOX_HEADS))).astype(BF16)
            b_f = jnp.pad(ev_b_f[e], (0, LANES - FOX_HEADS)).reshape(1, LANES)
            qT, qbT, k, kb, vT, u = _ev_in(x, row(mix_pre_g[layer]), w_main, w_f, b_f, tm_rows)
            attn = _fox_attn(qT, qbT, k, kb, vT, tq)
            w_out = ev_w_out[e].astype(BF16)
            x = _ev_out(x, attn, u, ev_pool_w[e].astype(BF16), row(ev_pool_scale[e]),
                        w_out[:FOX_WIDTH], w_out[FOX_WIDTH:], row(mix_post_g[layer]), tm_rows)
        ffn_params = (row(ffn_pre_g[layer]), ffn_w_gate[layer].astype(BF16), ffn_w_up[layer].astype(BF16),
                      ffn_w_down[layer].astype(BF16), row(ffn_post_g[layer]))
        if layer % 2 == 0:
            x = _ffn(x.reshape(B * S, D), *ffn_params, min(512, B * S)).reshape(B, S, D)
        else:
            o = layer // 2
            od_params = (row(mix_pre_g[layer]), od_w_in[o].astype(BF16), od_conv_w[o], row(od_conv_b[o]),
                         od_w_a[o].astype(BF16), row(od_b_a[o]), od_w_x[o].astype(BF16), row(od_b_x[o]),
                         row(od_lam[o]), od_w_out[o].astype(BF16), row(mix_post_g[layer]))
            x = _od_ffn(x, od_params, ffn_params, min(256, S))
    return x
```

```python
import functools
import math

import jax
import jax.numpy as jnp
import numpy as np
from jax import lax
from jax.experimental import pallas as pl
from jax.experimental.pallas import tpu as pltpu

F32 = jnp.float32
BF16 = jnp.bfloat16

RMS_EPS = 1e-6
FOX_HEADS = 8
FOX_HEAD_DIM = 64
FOX_WIDTH = FOX_HEADS * FOX_HEAD_DIM
POOL_WINDOWS = (2, 4, 8, 16)
POOL_GROUP_DIM = 128
POOL_WIDTH = len(POOL_WINDOWS) * POOL_GROUP_DIM
POOL_HALO = 16
LRU_HEADS = 4
LRU_HEAD_DIM = 256
CONV_WIDTH = 4
CONV_HALO = 8
LRU_C = 8.0
LOG2E = math.log2(math.e)

LANES = 128
SUBLANES = 8
HEAD_PAIR = 2 * FOX_HEAD_DIM
N_PAIRS = FOX_WIDTH // HEAD_PAIR
AUG = HEAD_PAIR + LANES
N_SPLIT = 3
ONES_LANE0 = N_SPLIT * FOX_HEADS
N_ONES = 4 * N_SPLIT
SHIFT_ROW0 = ONES_LANE0 + 2 * N_SPLIT
SAFE_SHIFT = 50.0
SHIFT_MARGIN = 1.02
F32_TINY = float(np.finfo(np.float32).tiny)
VMEM_LIMIT = 56 * 1024 * 1024


def _params(semantics):
    return pltpu.CompilerParams(dimension_semantics=semantics, vmem_limit_bytes=VMEM_LIMIT)


def _rmsnorm(x, g):
    ms = jnp.mean(x * x, axis=-1, keepdims=True)
    return x * lax.rsqrt(ms + RMS_EPS) * g


def _sigmoid(x):
    return 0.5 * jnp.tanh(0.5 * x) + 0.5


def _sqrt_unit(x):
    return x * lax.rsqrt(jnp.maximum(x, F32_TINY))


def _softplus(x):
    return jnp.maximum(x, 0.0) + jnp.log1p(jnp.exp(-jnp.abs(x)))


def _dot(a, b):
    return jnp.dot(a, b, preferred_element_type=F32)


def _split_pieces(x):
    pieces = []
    for _ in range(N_SPLIT):
        piece = x.astype(BF16).astype(F32)
        pieces.append(piece)
        x = x - piece
    return pieces


def _pack_split(x):
    pieces = _split_pieces(x)
    packed = pieces[0]
    for s in range(1, N_SPLIT):
        packed = packed + pltpu.roll(pieces[s], s * FOX_HEADS, axis=1)
    return packed


def _unpack_sum(x3):
    total = x3
    for s in range(1, N_SPLIT):
        total = total + pltpu.roll(x3, LANES - s * FOX_HEADS, axis=1)
    return total


def _query_bias_constants():
    place = np.zeros((LANES, N_PAIRS * LANES), np.float32)
    ones = np.zeros((1, N_PAIRS * LANES), np.float32)
    for pair in range(N_PAIRS):
        for hh in range(2):
            for s in range(N_SPLIT):
                lane = s * FOX_HEADS + 2 * pair + hh
                place[lane, pair * LANES + ONES_LANE0 + N_SPLIT * hh + s] = 1.0
                ones[0, pair * LANES + lane] = 1.0
    return jnp.asarray(place, BF16), jnp.asarray(ones)


def _head_rows(rows, head, hh):
    feat = (rows >= hh * FOX_HEAD_DIM) & (rows < (hh + 1) * FOX_HEAD_DIM)
    bias = rows - HEAD_PAIR
    sel = (bias >= 0) & (bias < ONES_LANE0) & (jnp.bitwise_and(bias, FOX_HEADS - 1) == head)
    c_t = (bias >= ONES_LANE0 + N_SPLIT * hh) & (bias < ONES_LANE0 + N_SPLIT * (hh + 1))
    return feat | sel | c_t


def _ev_in_kernel(x_ref, g_ref, wq_ref, wk_ref, wv_ref, wu_ref, wf_ref, bf_ref, tri_ref, qp_ref, qo_ref,
                  hs_ref, qT_ref, qbT_ref, k_ref, kb_ref, vT_ref, u_ref, kn_ref, csum_ref):
    tm = x_ref.shape[0]

    @pl.when(pl.program_id(1) == 0)
    def _():
        csum_ref[...] = jnp.zeros_like(csum_ref)

    h = _rmsnorm(x_ref[...], g_ref[...]).astype(BF16)
    lane = lax.broadcasted_iota(jnp.int32, (1, LANES), 1)
    is_head = lane < FOX_HEADS

    f = _dot(h, wf_ref[...]) + bf_ref[...]
    qT_ref[...] = (_dot(h, wq_ref[...]) * (LOG2E * FOX_HEAD_DIM ** -0.5)).T.astype(BF16)

    log_f = jnp.where(is_head, -_softplus(-f), 0.0)
    c3 = _dot(tri_ref[...], _pack_split(log_f).astype(BF16))
    k = _dot(h, wk_ref[...]).astype(BF16)
    k_ref[...] = k
    k_sq = k.astype(F32)
    head_sq = _dot((k_sq * k_sq).astype(BF16), hs_ref[...])
    kn_ref[...] = jnp.broadcast_to(jnp.max(head_sq, axis=0, keepdims=True), kn_ref.shape)

    c = jnp.where(is_head, _unpack_sum(c3), 0.0) + csum_ref[...]
    csum_ref[...] = c[tm - 1:tm, :]
    c_packed = _pack_split(c * LOG2E)
    key_ones = ((lane >= ONES_LANE0) & (lane < ONES_LANE0 + N_ONES)).astype(F32)
    kb_ref[...] = (key_ones - c_packed).astype(BF16)
    q_bias = _dot(c_packed.astype(BF16), qp_ref[...]) + qo_ref[...]

    vT = _dot(h, wv_ref[...]).T
    ones = jnp.ones((FOX_HEAD_DIM, tm), F32)
    v_rows = []
    for head in range(FOX_HEADS):
        v_head = vT[head * FOX_HEAD_DIM:(head + 1) * FOX_HEAD_DIM]
        v_rows += [v_head, ones] if head % 2 == 0 else [ones, v_head]
    vT_ref[...] = jnp.concatenate(v_rows, axis=0).astype(BF16)
    qbT_ref[...] = q_bias.T.astype(BF16)
    u_ref[...] = _dot(h, wu_ref[...])


def _ev_in(x, g, w_slabs, wf, bf, tm):
    B, S, D = x.shape
    tri = jnp.asarray(np.tril(np.ones((tm, tm), np.float32)), BF16)
    qp, qo = _query_bias_constants()
    head_sum = np.zeros((FOX_WIDTH, LANES), np.float32)
    head_sum[np.arange(FOX_WIDTH), np.arange(FOX_WIDTH) // FOX_HEAD_DIM] = 1.0
    head_sum = jnp.asarray(head_sum, BF16)
    const = lambda a: pl.BlockSpec(a.shape, lambda b, i: (0,) * a.ndim)
    rows = lambda width: pl.BlockSpec((None, tm, width), lambda b, i: (b, i, 0))
    cols = lambda height: pl.BlockSpec((None, height, tm), lambda b, i: (b, 0, i))
    return pl.pallas_call(
        _ev_in_kernel,
        grid=(B, S // tm),
        in_specs=[rows(D), const(g)] + [const(w) for w in w_slabs]
                 + [const(wf), const(bf), const(tri), const(qp), const(qo), const(head_sum)],
        out_specs=[cols(FOX_WIDTH), cols(N_PAIRS * LANES), rows(FOX_WIDTH), rows(LANES),
                   cols(FOX_HEADS * HEAD_PAIR), rows(POOL_WIDTH),
                   pl.BlockSpec((None, SUBLANES, LANES), lambda b, i: (b, i, 0))],
        out_shape=[
            jax.ShapeDtypeStruct((B, FOX_WIDTH, S), BF16),
            jax.ShapeDtypeStruct((B, N_PAIRS * LANES, S), BF16),
            jax.ShapeDtypeStruct((B, S, FOX_WIDTH), BF16),
            jax.ShapeDtypeStruct((B, S, LANES), BF16),
            jax.ShapeDtypeStruct((B, FOX_HEADS * HEAD_PAIR, S), BF16),
            jax.ShapeDtypeStruct((B, S, POOL_WIDTH), F32),
            jax.ShapeDtypeStruct((B, (S // tm) * SUBLANES, LANES), F32),
        ],
        scratch_shapes=[pltpu.VMEM((1, LANES), F32)],
        compiler_params=_params(("arbitrary", "arbitrary")),
        name="ev_in",
    )(x, g, *w_slabs, wf, bf, tri, qp, qo, head_sum)


NEG_FLOOR = -1e30


def _attn_kernel(qT_ref, qbT_ref, k_ref, kb_ref, vT_ref, kn_ref, o_ref, s_ref, p_ref, acc_ref, *, tq, tk):
    pair = pl.program_id(1)
    i = pl.program_id(2)
    q_feat = qT_ref[...]
    q_aug = jnp.concatenate([q_feat, qbT_ref[...]], axis=0)
    rows = lax.broadcasted_iota(jnp.int32, (AUG, tq), 0)
    qz = [jnp.where(_head_rows(rows, 2 * pair + hh, hh), q_aug, jnp.zeros_like(q_aug))
          for hh in range(2)]
    key_pos = lax.broadcasted_iota(jnp.int32, (tk, tq), 0)
    qry_pos = lax.broadcasted_iota(jnp.int32, (tk, tq), 1)
    causal = (key_pos <= qry_pos, key_pos + tk <= qry_pos)
    live = jnp.broadcast_to(i > 0, (tk, tq))
    acc_ref[...] = jnp.zeros_like(acc_ref)

    def key_block(blk):
        keys = pl.ds(pl.multiple_of(jnp.maximum(blk, 0) * tk, tk), tk)
        return jnp.concatenate([k_ref[keys, :], kb_ref[keys, :]], axis=1)

    def value_block(blk, hh):
        keys = pl.ds(pl.multiple_of(jnp.maximum(blk, 0) * tk, tk), tk)
        return vT_ref[hh * HEAD_PAIR:(hh + 1) * HEAD_PAIR, keys]

    q_sq = q_feat.astype(F32)
    q_sq = q_sq * q_sq
    lane = lax.broadcasted_iota(jnp.int32, (1, LANES), 1)
    k_sq_max = jnp.max(kn_ref[...], axis=0, keepdims=True)
    shift = []
    for hh in range(2):
        q_norm2 = jnp.sum(q_sq[hh * FOX_HEAD_DIM:(hh + 1) * FOX_HEAD_DIM], axis=0, keepdims=True)
        k_norm2 = jnp.max(jnp.where(lane == 2 * pair + hh, k_sq_max, 0.0), axis=1, keepdims=True)
        shift.append(SHIFT_MARGIN * jnp.sqrt(q_norm2 * k_norm2))
    safe = jnp.max(jnp.maximum(shift[0], shift[1])) < SAFE_SHIFT

    @pl.when(safe)
    def _():
        qs = []
        for hh in range(2):
            q = qz[hh]
            for s, piece in enumerate(_split_pieces(-shift[hh])):
                row = HEAD_PAIR + SHIFT_ROW0 + N_SPLIT * hh + s
                q = jnp.where(rows == row, jnp.broadcast_to(piece.astype(BF16), q.shape), q)
            qs.append(q)

        def weights(blk, slot, mask):
            k_blk = key_block(blk)
            for hh in range(2):
                sT = _dot(k_blk, qs[hh])
                if mask is not None:
                    sT = jnp.where(mask, sT, -jnp.inf)
                p_ref[slot, hh] = jnp.exp2(sT).astype(BF16)

        def accumulate(blk, slot):
            for hh in range(2):
                acc_ref[hh] += _dot(value_block(blk, hh), p_ref[slot, hh])

        def pair_step(u):
            weights(2 * u + 1, 1, None)
            accumulate(2 * u, 0)
            weights(2 * u + 2, 0, None)
            accumulate(2 * u + 1, 1)

        def quad_step(w, first):
            pair_step(first + 2 * w)
            pair_step(first + 2 * w + 1)
            return first

        weights(0, 0, live)
        n_pairs = jnp.maximum(i - 1, 0)
        odd = n_pairs % 2

        @pl.when(odd == 1)
        def _():
            pair_step(0)

        lax.fori_loop(0, n_pairs // 2, quad_step, odd)
        weights(2 * i - 1, 1, live)
        accumulate(2 * i - 2, 0)
        weights(2 * i, 0, causal[0])
        accumulate(2 * i - 1, 1)
        weights(2 * i + 1, 1, causal[1])
        accumulate(2 * i, 0)
        accumulate(2 * i + 1, 1)

    @pl.when(jnp.logical_not(safe))
    def _():
        _attn_running_max(key_block, value_block, qz, causal, live, i, s_ref, p_ref, acc_ref)

    out0 = acc_ref[0, :FOX_HEAD_DIM] / acc_ref[0, FOX_HEAD_DIM:FOX_HEAD_DIM + 1]
    out1 = acc_ref[1, FOX_HEAD_DIM:] / acc_ref[1, :1]
    o_ref[...] = jnp.concatenate([out0, out1], axis=0).T.astype(o_ref.dtype)


def _attn_running_max(key_block, value_block, qz, causal, live, i, s_ref, p_ref, acc_ref):
    tq = s_ref.shape[-1]

    def stage_a(blk, slot, mask):
        k_blk = key_block(blk)
        col_max = []
        for hh in range(2):
            sT = _dot(k_blk, qz[hh])
            if mask is not None:
                sT = jnp.where(mask, sT, -jnp.inf)
            s_ref[slot, hh] = sT
            col_max.append(jnp.max(sT, axis=0, keepdims=True))
        return tuple(col_max)

    def stage_b(slot, col_max, m_old):
        m, alpha = [], []
        for hh in range(2):
            m_new = jnp.maximum(m_old[hh], col_max[hh])
            alpha.append(jnp.exp2(m_old[hh] - m_new))
            p_ref[slot, hh] = jnp.exp2((s_ref[slot, hh] - m_new).astype(BF16))
            m.append(m_new)
        return tuple(m), tuple(alpha)

    def stage_c(blk, slot, alpha):
        for hh in range(2):
            acc_ref[hh] = alpha[hh] * acc_ref[hh] + _dot(value_block(blk, hh), p_ref[slot, hh])

    cm0 = stage_a(0, 0, live)
    cm1 = stage_a(1, 1, live)
    m, alpha0 = stage_b(0, cm0, tuple(jnp.full((1, tq), NEG_FLOOR, F32) for _ in range(2)))

    def pair_step(u, carry):
        m, cm1, alpha0 = carry
        cm0 = stage_a(2 * u, 0, None)
        m, alpha1 = stage_b(1, cm1, m)
        stage_c(2 * u - 2, 0, alpha0)
        cm1 = stage_a(2 * u + 1, 1, None)
        m, alpha0 = stage_b(0, cm0, m)
        stage_c(2 * u - 1, 1, alpha1)
        return m, cm1, alpha0

    m, cm1, alpha0 = lax.fori_loop(1, i, pair_step, (m, cm1, alpha0))

    cm0 = stage_a(2 * i, 0, causal[0])
    m, alpha1 = stage_b(1, cm1, m)
    stage_c(2 * i - 2, 0, alpha0)
    cm1 = stage_a(2 * i + 1, 1, causal[1])
    m, alpha0 = stage_b(0, cm0, m)
    stage_c(2 * i - 1, 1, alpha1)
    m, alpha1 = stage_b(1, cm1, m)
    stage_c(2 * i, 0, alpha0)
    stage_c(2 * i + 1, 1, alpha1)


def _fox_attn(qT, qbT, k, kb, vT, kn, tq):
    B, S, _ = k.shape
    tk = tq // 2
    kern = functools.partial(_attn_kernel, tq=tq, tk=tk)
    return pl.pallas_call(
        kern,
        grid=(B, N_PAIRS, S // tq),
        in_specs=[
            pl.BlockSpec((None, HEAD_PAIR, tq), lambda b, p, i: (b, p, i)),
            pl.BlockSpec((None, LANES, tq), lambda b, p, i: (b, p, i)),
            pl.BlockSpec((None, S, HEAD_PAIR), lambda b, p, i: (b, 0, p)),
            pl.BlockSpec((None, S, LANES), lambda b, p, i: (b, 0, 0)),
            pl.BlockSpec((None, 2 * HEAD_PAIR, S), lambda b, p, i: (b, p, 0)),
            pl.BlockSpec((None,) + kn.shape[1:], lambda b, p, i: (b, 0, 0)),
        ],
        out_specs=pl.BlockSpec((None, tq, HEAD_PAIR), lambda b, p, i: (b, i, p)),
        out_shape=jax.ShapeDtypeStruct((B, S, FOX_WIDTH), BF16),
        scratch_shapes=[pltpu.VMEM((2, 2, tk, tq), F32), pltpu.VMEM((2, 2, tk, tq), BF16),
                        pltpu.VMEM((2, HEAD_PAIR, tq), F32)],
        compiler_params=_params(("parallel", "parallel", "parallel")),
        name="fox_attn",
    )(qT, qbT, k, kb, vT, kn)


def _ev_out_kernel(x_ref, attn_ref, u_ref, uprev_ref, pw_ref, ps_ref, woa_ref, wop_ref, g_ref,
                   o_ref, *, tm):
    i = pl.program_id(1)
    u = u_ref[...]
    halo = jnp.where(i > 0, uprev_ref[...], 0.0)
    ucat = jnp.concatenate([halo, u], axis=0)
    frames = (i * tm + 1 + lax.broadcasted_iota(jnp.int32, (tm, 1), 0)).astype(F32)
    mixed = []
    for g, w in enumerate(POOL_WINDOWS):
        ug = ucat[:, g * POOL_GROUP_DIM:(g + 1) * POOL_GROUP_DIM]
        wsum = ug[POOL_HALO:]
        for back in range(1, w):
            wsum = wsum + ug[POOL_HALO - back:POOL_HALO - back + tm]
        mean = wsum / jnp.minimum(frames, float(w))
        pooled = (mean - ug[POOL_HALO:]).astype(BF16)
        mixed.append(_dot(pooled, pw_ref[g]))
    pool = jnp.concatenate(mixed, axis=1) * ps_ref[...]
    m = _dot(attn_ref[...], woa_ref[...]) + _dot(pool.astype(BF16), wop_ref[...])
    o_ref[...] = x_ref[...] + _rmsnorm(m, g_ref[...])


def _ev_out(x, attn, u, pw, ps, woa, wop, g, tm):
    B, S, D = x.shape
    halo_blocks = tm // POOL_HALO
    kern = functools.partial(_ev_out_kernel, tm=tm)
    return pl.pallas_call(
        kern,
        grid=(B, S // tm),
        in_specs=[
            pl.BlockSpec((None, tm, D), lambda b, i: (b, i, 0)),
            pl.BlockSpec((None, tm, FOX_WIDTH), lambda b, i: (b, i, 0)),
            pl.BlockSpec((None, tm, POOL_WIDTH), lambda b, i: (b, i, 0)),
            pl.BlockSpec((None, POOL_HALO, POOL_WIDTH),
                         lambda b, i: (b, jnp.maximum(i * halo_blocks - 1, 0), 0)),
            pl.BlockSpec(pw.shape, lambda b, i: (0, 0, 0)),
            pl.BlockSpec((1, POOL_WIDTH), lambda b, i: (0, 0)),
            pl.BlockSpec(woa.shape, lambda b, i: (0, 0)),
            pl.BlockSpec(wop.shape, lambda b, i: (0, 0)),
            pl.BlockSpec((1, D), lambda b, i: (0, 0)),
        ],
        out_specs=pl.BlockSpec((None, tm, D), lambda b, i: (b, i, 0)),
        out_shape=jax.ShapeDtypeStruct((B, S, D), F32),
        compiler_params=_params(("parallel", "parallel")),
        name="ev_out",
    )(x, attn, u, u, pw, ps, woa, wop, g)


def _ffn_stages(x, gpre_ref, wg_ref, wu_ref, wd_ref, gpost_ref):
    h = _rmsnorm(x, gpre_ref[...]).astype(BF16)
    gate = _dot(h, wg_ref[...])
    yield None
    up = _dot(h, wu_ref[...])
    yield None
    act = (gate * _sigmoid(gate) * up).astype(BF16)
    y = _dot(act, wd_ref[...])
    yield None
    yield x + _rmsnorm(y, gpost_ref[...])


def _ffn_kernel(x_ref, gpre_ref, wg_ref, wu_ref, wd_ref, gpost_ref, o_ref):
    *_, out = _ffn_stages(x_ref[...], gpre_ref, wg_ref, wu_ref, wd_ref, gpost_ref)
    o_ref[...] = out


def _ffn(x, gpre, wg, wu, wd, gpost, tm):
    T, D = x.shape
    dff = wg.shape[1]
    resident = dict(pipeline_mode=pl.Buffered(1))
    return pl.pallas_call(
        _ffn_kernel,
        grid=(T // tm,),
        in_specs=[
            pl.BlockSpec((tm, D), lambda i: (i, 0)),
            pl.BlockSpec((1, D), lambda i: (0, 0)),
            pl.BlockSpec((D, dff), lambda i: (0, 0), **resident),
            pl.BlockSpec((D, dff), lambda i: (0, 0), **resident),
            pl.BlockSpec((dff, D), lambda i: (0, 0), **resident),
            pl.BlockSpec((1, D), lambda i: (0, 0)),
        ],
        out_specs=pl.BlockSpec((tm, D), lambda i: (i, 0)),
        out_shape=jax.ShapeDtypeStruct((T, D), F32),
        compiler_params=_params(("parallel",)),
        name="ffn",
    )(x, gpre, wg, wu, wd, gpost)


def _od_stages(x, gpre_ref, win_ref, cw_ref, cb_ref, wa_ref, ba_ref, wx_ref, bx_ref,
               lam_ref, wout_ref, gpost_ref, tail_ref, state_ref, a_ref, b_ref):
    tm = x.shape[0]
    width = state_ref.shape[1]
    h = _rmsnorm(x, gpre_ref[...]).astype(BF16)
    proj = _dot(h, win_ref[...])
    yield None
    gate = proj[:, :width]
    xr = proj[:, width:]

    xcat = jnp.concatenate([tail_ref[...], xr], axis=0)
    tail_ref[...] = xr[tm - CONV_HALO:]
    cw = cw_ref[...]
    xc = cb_ref[...] + cw[CONV_WIDTH - 1:CONV_WIDTH] * xr
    for back in range(1, CONV_WIDTH):
        tap = cw[CONV_WIDTH - 1 - back:CONV_WIDTH - back]
        xc = xc + tap * xcat[CONV_HALO - back:CONV_HALO - back + tm]

    xcb = xc.astype(BF16)
    r_parts, i_parts = [], []
    for hd in range(LRU_HEADS):
        sl = slice(hd * LRU_HEAD_DIM, (hd + 1) * LRU_HEAD_DIM)
        r_parts.append(_dot(xcb[:, sl], wa_ref[hd]))
        i_parts.append(_dot(xcb[:, sl], wx_ref[hd]))
    yield None
    r = _sigmoid(jnp.concatenate(r_parts, axis=1) + ba_ref[...])
    ig = _sigmoid(jnp.concatenate(i_parts, axis=1) + bx_ref[...])
    log_a = (-LRU_C * _softplus(-lam_ref[...])) * r
    a = jnp.exp(log_a)
    th = jnp.tanh(log_a)
    b = _sqrt_unit(-2.0 * th / (1.0 - th)) * (ig * xc)

    groups = tm // SUBLANES
    grp = lax.broadcasted_iota(jnp.int32, (groups, LANES), 0)
    pos = lambda ref, c, r: ref.at[c, pl.ds(r, groups, stride=SUBLANES), :]
    y_cols = []
    for c in range(width // LANES):
        lanes = slice(c * LANES, (c + 1) * LANES)
        a_ref[c] = a[:, lanes]
        b_ref[c] = b[:, lanes]
        a_pos = [pos(a_ref, c, r)[...] for r in range(SUBLANES)]
        b_pos = [pos(b_ref, c, r)[...] for r in range(SUBLANES)]
        for r in range(1, SUBLANES):
            b_pos[r] = a_pos[r] * b_pos[r - 1] + b_pos[r]
            a_pos[r] = a_pos[r] * a_pos[r - 1]
        ga, gb = a_pos[SUBLANES - 1], b_pos[SUBLANES - 1]
        d = 1
        while d < groups:
            keep = grp >= d
            gb = jnp.where(keep, ga * pltpu.roll(gb, d, axis=0) + gb, gb)
            ga = jnp.where(keep, ga * pltpu.roll(ga, d, axis=0), ga)
            d *= 2
        state = state_ref[:, lanes]
        h_end = ga * state + gb
        carry_in = jnp.where(grp >= 1, pltpu.roll(h_end, 1, axis=0), state)
        state_ref[:, lanes] = h_end[groups - 1:groups]
        for r in range(SUBLANES):
            pos(b_ref, c, r)[...] = a_pos[r] * carry_in + b_pos[r]
        y_cols.append(b_ref[c])
    y = jnp.concatenate(y_cols, axis=1)

    gelu = 0.5 * gate * (1.0 + jnp.tanh(math.sqrt(2.0 / math.pi) * (gate + 0.044715 * (gate * gate * gate))))
    m = _dot((gelu * y).astype(BF16), wout_ref[...])
    yield None
    yield x + _rmsnorm(m, gpost_ref[...])


def _od_ffn_kernel(x_ref, *refs, n_seq, n_tiles):
    od_refs, ffn_refs = refs[:11], refs[11:16]
    o_ref, mid_ref, tail_ref, state_ref, a_ref, b_ref = refs[16:]
    t = pl.program_id(0)

    @pl.when(t == 0)
    def _():
        mid_ref[...] = jnp.zeros_like(mid_ref)

    @pl.when(t % n_seq == 0)
    def _():
        tail_ref[...] = jnp.zeros_like(tail_ref)
        state_ref[...] = jnp.zeros_like(state_ref)

    ffn = _ffn_stages(mid_ref[(t + 1) % 2], *ffn_refs)
    od = _od_stages(x_ref[...], *od_refs, tail_ref, state_ref, a_ref, b_ref)
    for gen in (od, ffn, od, ffn, ffn, od):
        next(gen)
    o_ref[...] = next(ffn)
    mid_ref[t % 2] = next(od)


def _od_ffn(x, od_params, ffn_params, tm):
    B, S, D = x.shape
    n_seq = S // tm
    n_tiles = B * n_seq
    width = od_params[-2].shape[0]
    kern = functools.partial(_od_ffn_kernel, n_seq=n_seq, n_tiles=n_tiles)
    resident = lambda a: pl.BlockSpec(a.shape, lambda t: (0,) * a.ndim, pipeline_mode=pl.Buffered(1))
    params = list(od_params) + list(ffn_params)
    out = pl.pallas_call(
        kern,
        grid=(n_tiles + 1,),
        in_specs=[pl.BlockSpec((tm, D), lambda t: (jnp.minimum(t, n_tiles - 1), 0))]
                 + [resident(a) for a in params],
        out_specs=pl.BlockSpec((tm, D), lambda t: (jnp.maximum(t - 1, 0), 0)),
        out_shape=jax.ShapeDtypeStruct((B * S, D), F32),
        scratch_shapes=[pltpu.VMEM((2, tm, D), F32), pltpu.VMEM((CONV_HALO, width), F32),
                        pltpu.VMEM((1, width), F32), pltpu.VMEM((width // LANES, tm, LANES), F32),
                        pltpu.VMEM((width // LANES, tm, LANES), F32)],
        compiler_params=_params(("arbitrary",)),
        name="od_ffn",
    )(x.reshape(B * S, D), *params)
    return out.reshape(B, S, D)


def kernel(x, mix_pre_g, mix_post_g, ffn_pre_g, ffn_post_g, ffn_w_gate, ffn_w_up, ffn_w_down,
           ev_w_in, ev_b_f, ev_pool_w, ev_pool_scale, ev_w_out,
           od_w_in, od_conv_w, od_conv_b, od_w_a, od_b_a, od_w_x, od_b_x, od_lam, od_w_out):
    B, S, D = x.shape
    depth = mix_pre_g.shape[0]
    row = lambda v: v.reshape(1, -1)
    tm_rows = min(512, S)
    tq = min(512, S)

    for layer in range(depth):
        if layer % 2 == 0:
            e = layer // 2
            w_in = ev_w_in[e]
            qkv_w = 3 * FOX_WIDTH
            w_main = [w_in[:, n * FOX_WIDTH:(n + 1) * FOX_WIDTH].astype(BF16) for n in range(3)]
            w_main.append(w_in[:, qkv_w + FOX_HEADS:].astype(BF16))
            w_f = jnp.pad(w_in[:, qkv_w:qkv_w + FOX_HEADS], ((0, 0), (0, LANES - FOX_HEADS))).astype(BF16)
            b_f = jnp.pad(ev_b_f[e], (0, LANES - FOX_HEADS)).reshape(1, LANES)
            qT, qbT, k, kb, vT, u, kn = _ev_in(x, row(mix_pre_g[layer]), w_main, w_f, b_f, tm_rows)
            attn = _fox_attn(qT, qbT, k, kb, vT, kn, tq)
            w_out = ev_w_out[e].astype(BF16)
            x = _ev_out(x, attn, u, ev_pool_w[e].astype(BF16), row(ev_pool_scale[e]),
                        w_out[:FOX_WIDTH], w_out[FOX_WIDTH:], row(mix_post_g[layer]), tm_rows)
        ffn_params = (row(ffn_pre_g[layer]), ffn_w_gate[layer].astype(BF16), ffn_w_up[layer].astype(BF16),
                      ffn_w_down[layer].astype(BF16), row(ffn_post_g[layer]))
        if layer % 2 == 0:
            x = _ffn(x.reshape(B * S, D), *ffn_params, min(512, B * S)).reshape(B, S, D)
        else:
            o = layer // 2
            od_params = (row(mix_pre_g[layer]), od_w_in[o].astype(BF16), od_conv_w[o], row(od_conv_b[o]),
                         od_w_a[o].astype(BF16), row(od_b_a[o]), od_w_x[o].astype(BF16), row(od_b_x[o]),
                         row(od_lam[o]), od_w_out[o].astype(BF16), row(mix_post_g[layer]))
            x = _od_ffn(x, od_params, ffn_params, min(256, S))
    return x
```
